```python
import jax, jax.numpy as jnp
from jax import lax
import numpy as np

D_MODEL = 1024
BATCH = 2
SEQ = 8192
DEPTH = 1

GRID_W = 64
CTX_LEN = 256
D_MIX = D_MODEL
RET_HEADS = 4
RET_WIDTH = D_MIX // 2
RET_DK = RET_WIDTH // RET_HEADS
RET_DV = RET_WIDTH // RET_HEADS
GLA_HEADS = 4
GLA_WIDTH = D_MIX - RET_WIDTH
GLA_DV = GLA_WIDTH // GLA_HEADS
GLA_DK = GLA_DV // 2
GLA_RANK = 16
GLA_GATE_NORM = 16.0
CHUNK = 64
ROPE_BASE = 10000.0
N_KEYS = 128
N_EXPERTS = N_KEYS * N_KEYS
PEER_HEADS = 8
PEER_TOPK = 16
PEER_DQ = 256
PEER_BLOCK = 128
N_MOD = 6
EPS = 1e-6
IN_SIZES = (RET_WIDTH, RET_WIDTH, RET_WIDTH, RET_WIDTH,
            GLA_HEADS * GLA_DK, GLA_HEADS * GLA_DK, GLA_WIDTH, GLA_WIDTH,
            GLA_RANK, GLA_RANK)
IN_COLS = sum(IN_SIZES)

kernel_name = "hymba_retention_gla_peer_dit"

F32 = jnp.float32


def rmsnorm(x, w):
    xf = x.astype(F32)
    y = xf * lax.rsqrt(jnp.mean(xf * xf, axis=-1, keepdims=True) + EPS)
    return (y * w.astype(F32)).astype(x.dtype)


def head_rmsnorm(o):
    o = o * lax.rsqrt(jnp.mean(o * o, axis=-1, keepdims=True) + EPS)
    B, H, L, d = o.shape
    return o.transpose(0, 2, 1, 3).reshape(B, L, H * d)


def modulate(h, shift, scale):
    return h * (1 + scale) + shift


def to_heads(t, n_heads):
    B, L, W = t.shape
    return t.reshape(B, L, n_heads, W // n_heads).transpose(0, 2, 1, 3)


def flip_seq(t):
    return jnp.flip(t, axis=2)


def axial_rope(rows, cols):
    r, c = jnp.meshgrid(jnp.arange(rows, dtype=F32), jnp.arange(cols, dtype=F32), indexing="ij")
    n_freq = RET_DK // 4
    inv = ROPE_BASE ** (-jnp.arange(n_freq, dtype=F32) / n_freq)
    ang = jnp.concatenate([r.reshape(-1, 1) * inv, c.reshape(-1, 1) * inv], axis=-1)
    return jnp.cos(ang), jnp.sin(ang)


def apply_rope(t, cos, sin):
    t1, t2 = jnp.split(t, 2, axis=-1)
    return jnp.concatenate([t1 * cos - t2 * sin, t2 * cos + t1 * sin], axis=-1)


def to_chunks(t):
    B, H, L, d = t.shape
    return t.reshape(B, H, L // CHUNK, CHUNK, d).transpose(2, 0, 1, 3, 4)


def from_chunks(t):
    n, B, H, C, d = t.shape
    return t.transpose(1, 2, 0, 3, 4).reshape(B, H, n * C, d)


def retention_scan(q, k, v, log_gamma, s0):
    idx = jnp.arange(CHUNK, dtype=F32)
    rel = idx[:, None] - idx[None, :]
    lg = log_gamma[:, None, None]
    decay_mask = jnp.where(rel >= 0, jnp.exp(lg * jnp.maximum(rel, 0.0)), 0.0)
    q_dec = jnp.exp(log_gamma[:, None] * (idx + 1.0))[None, :, :, None]
    k_dec = jnp.exp(log_gamma[:, None] * (CHUNK - 1.0 - idx))[None, :, :, None]
    chunk_dec = jnp.exp(log_gamma * CHUNK)[None, :, None, None]

    def step(s, inp):
        qc, kc, vc = inp
        scores = jnp.einsum("bhid,bhjd->bhij", qc, kc) * decay_mask
        out = (jnp.einsum("bhij,bhjv->bhiv", scores, vc)
               + jnp.einsum("bhid,bhdv->bhiv", qc * q_dec, s))
        s = chunk_dec * s + jnp.einsum("bhjd,bhjv->bhdv", kc * k_dec, vc)
        return s, out

    s, out = lax.scan(step, s0, (to_chunks(q), to_chunks(k), to_chunks(v)))
    return from_chunks(out), s


def gla_scan(q, k, v, log_a, s0):
    causal = (jnp.arange(CHUNK)[:, None] >= jnp.arange(CHUNK)[None, :])

    def step(s, inp):
        qc, kc, vc, ac = inp
        G = jnp.cumsum(ac, axis=2)
        G_last = G[:, :, -1:, :]
        q_t = qc * jnp.exp(G)
        k_t = kc * jnp.exp(-G)
        scores = jnp.where(causal, jnp.einsum("bhid,bhjd->bhij", q_t, k_t), 0.0)
        out = (jnp.einsum("bhij,bhjv->bhiv", scores, vc)
               + jnp.einsum("bhid,bhdv->bhiv", q_t, s))
        s = (jnp.exp(G_last[:, :, 0, :])[..., None] * s
             + jnp.einsum("bhjd,bhjv->bhdv", kc * jnp.exp(G_last - G), vc))
        return s, out

    s, out = lax.scan(step, s0, (to_chunks(q), to_chunks(k), to_chunks(v), to_chunks(log_a)))
    return from_chunks(out), s


def retention_final_state(k, v, log_gamma):
    L = k.shape[2]
    w = jnp.exp(log_gamma[:, None] * (L - 1.0 - jnp.arange(L, dtype=F32)))
    return jnp.einsum("bhld,bhle->bhde", k * w[None, :, :, None], v)


def gla_final_state(k, v, log_a):
    G = jnp.cumsum(log_a, axis=2)
    return jnp.einsum("bhld,bhle->bhde", k * jnp.exp(G[:, :, -1:, :] - G), v)


def project(h, w_in, gk_up_f, gk_bias_f, gk_up_b, gk_bias_b, rope):
    p = (h @ w_in).astype(F32)
    splits = [int(s) for s in np.cumsum(IN_SIZES)[:-1]]
    rq, rk, rv, rg, gq, gk, gv, gg, lr_f, lr_b = jnp.split(p, splits, axis=-1)
    rq = to_heads(rq, RET_HEADS)
    rk = to_heads(rk, RET_HEADS) * (RET_DK ** -0.5)
    rv = to_heads(rv, RET_HEADS)
    if rope is not None:
        rq = apply_rope(rq, *rope)
        rk = apply_rope(rk, *rope)
    gq = to_heads(gq, GLA_HEADS) * (GLA_DK ** -0.5)
    gk = to_heads(gk, GLA_HEADS)
    gv = to_heads(gv, GLA_HEADS)
    la_f = to_heads(jax.nn.log_sigmoid(lr_f @ gk_up_f.astype(F32) + gk_bias_f.astype(F32)) / GLA_GATE_NORM, GLA_HEADS)
    la_b = to_heads(jax.nn.log_sigmoid(lr_b @ gk_up_b.astype(F32) + gk_bias_b.astype(F32)) / GLA_GATE_NORM, GLA_HEADS)
    return (rq, rk, rv, rg, gq, gk, gv, gg, la_f, la_b)


def bidir_mix(feats, lg_f, lg_b, states):
    rq, rk, rv, rg, gq, gk, gv, gg, la_f, la_b = feats
    s_rf, s_rb, s_gf, s_gb = states
    o_rf, s_rf = retention_scan(rq, rk, rv, lg_f, s_rf)
    o_rb, s_rb = retention_scan(flip_seq(rq), flip_seq(rk), flip_seq(rv), lg_b, s_rb)
    o_gf, s_gf = gla_scan(gq, gk, gv, la_f, s_gf)
    o_gb, s_gb = gla_scan(flip_seq(gq), flip_seq(gk), flip_seq(gv), flip_seq(la_b), s_gb)
    ret = head_rmsnorm(o_rf + flip_seq(o_rb)) * jax.nn.silu(rg)
    gla = head_rmsnorm(o_gf + flip_seq(o_gb)) * jax.nn.silu(gg)
    return jnp.concatenate([ret, gla], axis=-1), (s_rf, s_rb, s_gf, s_gb)


def context_final_states(feats, lg_f, lg_b):
    _, rk, rv, _, _, gk, gv, _, la_f, la_b = feats
    return (retention_final_state(rk, rv, lg_f),
            retention_final_state(flip_seq(rk), flip_seq(rv), lg_b),
            gla_final_state(gk, gv, la_f),
            gla_final_state(flip_seq(gk), flip_seq(gv), flip_seq(la_b)))


def zero_states(b):
    return (jnp.zeros((b, RET_HEADS, RET_DK, RET_DV), F32),
            jnp.zeros((b, RET_HEADS, RET_DK, RET_DV), F32),
            jnp.zeros((b, GLA_HEADS, GLA_DK, GLA_DV), F32),
            jnp.zeros((b, GLA_HEADS, GLA_DK, GLA_DV), F32))


def peer_ffn(h, w_q, k1, k2, down, up):
    B, L, D = h.shape
    half = PEER_DQ // 2
    blocks = h.reshape(-1, PEER_BLOCK, D)

    def block(xb):
        q = (xb @ w_q).astype(F32).reshape(PEER_BLOCK, PEER_HEADS, 2, half)
        s1 = jnp.einsum("thd,hnd->thn", q[:, :, 0], k1.astype(F32))
        s2 = jnp.einsum("thd,hnd->thn", q[:, :, 1], k2.astype(F32))
        v1, i1 = lax.top_k(s1, PEER_TOPK)
        v2, i2 = lax.top_k(s2, PEER_TOPK)
        cand_s = (v1[..., :, None] + v2[..., None, :]).reshape(PEER_BLOCK, PEER_HEADS, PEER_TOPK * PEER_TOPK)
        cand_i = (i1[..., :, None] * N_KEYS + i2[..., None, :]).reshape(PEER_BLOCK, PEER_HEADS, PEER_TOPK * PEER_TOPK)
        top_s, pos = lax.top_k(cand_s, PEER_TOPK)
        idx = jnp.take_along_axis(cand_i, pos, axis=-1)
        g = jax.nn.softmax(top_s, axis=-1)
        u = down[idx]
        act = jax.nn.gelu(jnp.einsum("thkd,td->thk", u, xb).astype(F32), approximate=False)
        return jnp.einsum("thk,thkd->td", (g * act).astype(xb.dtype), up[idx])

    return lax.map(block, blocks).reshape(B, L, D)


def trunk_layer(x, xc, c, c_ctx, w_mod, b_mod, norm1_w, norm2_w, w_in,
                ret_decay_f, ret_decay_b, gk_up_f, gk_bias_f, gk_up_b, gk_bias_b,
                w_out, peer_w_q, peer_k1, peer_k2, peer_down, peer_up, rope, update_ctx):
    mod = (jax.nn.silu(c) @ w_mod + b_mod)[:, None, :]
    mod_c = (jax.nn.silu(c_ctx) @ w_mod + b_mod)[None, None, :]
    sh1, sc1, g1, sh2, sc2, g2 = jnp.split(mod, N_MOD, axis=-1)
    csh1, csc1, cg1, csh2, csc2, cg2 = jnp.split(mod_c, N_MOD, axis=-1)
    lg_f = jax.nn.log_sigmoid(ret_decay_f.astype(F32))
    lg_b = jax.nn.log_sigmoid(ret_decay_b.astype(F32))

    feats_c = project(modulate(rmsnorm(xc, norm1_w), csh1, csc1),
                      w_in, gk_up_f, gk_bias_f, gk_up_b, gk_bias_b, None)
    if update_ctx:
        yc, ctx_states = bidir_mix(feats_c, lg_f, lg_b, zero_states(xc.shape[0]))
    else:
        ctx_states = context_final_states(feats_c, lg_f, lg_b)

    feats = project(modulate(rmsnorm(x, norm1_w), sh1, sc1),
                    w_in, gk_up_f, gk_bias_f, gk_up_b, gk_bias_b, rope)
    y, _ = bidir_mix(feats, lg_f, lg_b, ctx_states)
    x = x + g1 * (y.astype(x.dtype) @ w_out)
    x = x + g2 * peer_ffn(modulate(rmsnorm(x, norm2_w), sh2, sc2),
                          peer_w_q, peer_k1, peer_k2, peer_down, peer_up)
    if update_ctx:
        xc = xc + cg1 * (yc.astype(xc.dtype) @ w_out)
        xc = xc + cg2 * peer_ffn(modulate(rmsnorm(xc, norm2_w), csh2, csc2),
                                 peer_w_q, peer_k1, peer_k2, peer_down, peer_up)
    return x, xc


def setup_inputs(seed: int = 0) -> dict:
    key = jax.random.key(seed)
    ks = jax.random.split(key, 24)

    def nrm(k, shape, s):
        return jax.random.normal(k, shape, F32) * s

    base_logit = jnp.asarray(np.log(2.0 ** (5 + np.arange(RET_HEADS)) - 1.0), F32)[None, :]
    return {
        "x": nrm(ks[0], (BATCH, SEQ, D_MODEL), 1.0),
        "c": nrm(ks[1], (BATCH, D_MODEL), 1.0),
        "ctx": nrm(ks[2], (BATCH, CTX_LEN, D_MODEL), 1.0),
        "c_ctx": nrm(ks[3], (D_MODEL,), 1.0),
        "w_mod": nrm(ks[4], (DEPTH, D_MODEL, N_MOD * D_MODEL), 0.5 * D_MODEL ** -0.5),
        "b_mod": nrm(ks[5], (DEPTH, N_MOD * D_MODEL), 0.02),
        "norm1_w": 1.0 + nrm(ks[6], (DEPTH, D_MODEL), 0.02),
        "norm2_w": 1.0 + nrm(ks[7], (DEPTH, D_MODEL), 0.02),
        "w_in": nrm(ks[8], (DEPTH, D_MODEL, IN_COLS), D_MODEL ** -0.5),
        "ret_decay_f": base_logit + nrm(ks[9], (DEPTH, RET_HEADS), 0.05),
        "ret_decay_b": base_logit + nrm(ks[10], (DEPTH, RET_HEADS), 0.05),
        "gla_gk_up_f": nrm(ks[11], (DEPTH, GLA_RANK, GLA_HEADS * GLA_DK), GLA_RANK ** -0.5),
        "gla_gk_bias_f": nrm(ks[12], (DEPTH, GLA_HEADS * GLA_DK), 0.1),
        "gla_gk_up_b": nrm(ks[13], (DEPTH, GLA_RANK, GLA_HEADS * GLA_DK), GLA_RANK ** -0.5),
        "gla_gk_bias_b": nrm(ks[14], (DEPTH, GLA_HEADS * GLA_DK), 0.1),
        "w_out": nrm(ks[15], (DEPTH, D_MIX, D_MODEL), D_MIX ** -0.5),
        "peer_w_q": nrm(ks[16], (DEPTH, D_MODEL, PEER_HEADS * PEER_DQ), D_MODEL ** -0.5),
        "peer_k1": nrm(ks[17], (DEPTH, PEER_HEADS, N_KEYS, PEER_DQ // 2), (PEER_DQ // 2) ** -0.5),
        "peer_k2": nrm(ks[18], (DEPTH, PEER_HEADS, N_KEYS, PEER_DQ // 2), (PEER_DQ // 2) ** -0.5),
        "peer_down": nrm(ks[19], (DEPTH, N_EXPERTS, D_MODEL), D_MODEL ** -0.5),
        "peer_up": nrm(ks[20], (DEPTH, N_EXPERTS, D_MODEL), 0.5),
        "norm_f_w": 1.0 + nrm(ks[21], (D_MODEL,), 0.02),
    }


def reference(x, c, ctx, c_ctx, w_mod, b_mod, norm1_w, norm2_w, w_in,
              ret_decay_f, ret_decay_b, gla_gk_up_f, gla_gk_bias_f, gla_gk_up_b, gla_gk_bias_b,
              w_out, peer_w_q, peer_k1, peer_k2, peer_down, peer_up, norm_f_w):
    L = x.shape[1]
    ROWS = L // GRID_W
    rope = axial_rope(ROWS, GRID_W)
    xc = ctx
    for li in range(DEPTH):
        x, xc = trunk_layer(x, xc, c, c_ctx, w_mod[li], b_mod[li], norm1_w[li], norm2_w[li], w_in[li],
                            ret_decay_f[li], ret_decay_b[li], gla_gk_up_f[li], gla_gk_bias_f[li],
                            gla_gk_up_b[li], gla_gk_bias_b[li], w_out[li], peer_w_q[li],
                            peer_k1[li], peer_k2[li], peer_down[li], peer_up[li], rope,
                            li < DEPTH - 1)
    return rmsnorm(x, norm_f_w)
```

```python
import functools

import jax
import jax.numpy as jnp
from jax import lax
from jax.experimental import pallas as pl
from jax.experimental.pallas import tpu as pltpu

F32 = jnp.float32
BF16 = jnp.bfloat16
HIGHEST = lax.Precision.HIGHEST

D_MODEL = 1024
GRID_W = 64
RET_HEADS = 4
RET_WIDTH = 512
RET_DK = 128
GLA_HEADS = 4
GLA_WIDTH = 512
GLA_DV = 128
GLA_DK = 64
GLA_KW = GLA_HEADS * GLA_DK
GLA_RANK = 16
GLA_GATE_NORM = 16.0
CHUNK = 64
ROPE_BASE = 10000.0
N_KEYS = 128
PEER_HEADS = 8
PEER_TOPK = 16
PEER_DQ = 256
N_MOD = 6
EPS = 1e-6
MAIN_COLS = 3584
LANES = 128
VMEM_LIMIT = 56 * 1024 * 1024


def _cparams(sem):
    return pltpu.CompilerParams(dimension_semantics=sem, vmem_limit_bytes=VMEM_LIMIT)


def _silu(v):
    return v / (1.0 + jnp.exp(-v))


def _log_sigmoid(z):
    return jnp.minimum(z, 0.0) - jnp.log(1.0 + jnp.exp(-jnp.abs(z)))


def _dot(a, b):
    return jnp.dot(a, b, preferred_element_type=F32)


def _dot_nt(a, b):
    return lax.dot_general(a, b, (((1,), (1,)), ((), ())), preferred_element_type=F32)


def _mod_kernel(c_ref, w_ref, b_ref, o_ref):
    a = _silu(c_ref[...])
    o_ref[...] = jnp.dot(a, w_ref[...], precision=HIGHEST, preferred_element_type=F32) + b_ref[...]


def _modulation(cc, w_mod, b_mod):
    n = w_mod.shape[1]
    tn = 1536
    return pl.pallas_call(
        _mod_kernel,
        grid=(n // tn,),
        in_specs=[pl.BlockSpec((8, D_MODEL), lambda j: (0, 0)),
                  pl.BlockSpec((D_MODEL, tn), lambda j: (0, j)),
                  pl.BlockSpec((1, tn), lambda j: (0, j))],
        out_specs=pl.BlockSpec((8, tn), lambda j: (0, j)),
        out_shape=jax.ShapeDtypeStruct((8, n), F32),
        compiler_params=_cparams(("arbitrary",)),
        name="modulation",
    )(cc, w_mod, b_mod)


def _inproj_kernel(x_ref, sh_ref, sc_ref, nw_ref, wm_ref, wl_ref, up_ref, gb_ref, cos_ref, sin_ref,
                   rq_ref, rk_ref, rv_ref, rg_ref, gq_ref, gk_ref, gv_ref, gg_ref, laf_ref, lab_ref):
    x = x_ref[0]
    ms = jnp.mean(x * x, axis=-1, keepdims=True)
    h = x * lax.rsqrt(ms + EPS) * nw_ref[...]
    h = h * (1.0 + sc_ref[0]) + sh_ref[0]
    p = _dot(h.astype(BF16), wm_ref[...])
    cos = cos_ref[...]
    sin = sin_ref[...]
    k_scale = RET_DK ** -0.5
    for hh in range(RET_HEADS):
        lo = hh * RET_DK
        q = p[:, lo:lo + RET_DK]
        k = p[:, RET_WIDTH + lo:RET_WIDTH + lo + RET_DK] * k_scale
        rq_ref[0, :, lo:lo + RET_DK] = q * cos + pltpu.roll(q, RET_DK // 2, axis=1) * sin
        rk_ref[0, :, lo:lo + RET_DK] = k * cos + pltpu.roll(k, RET_DK // 2, axis=1) * sin
    rv_ref[0] = p[:, 1024:1536]
    rg_ref[0] = p[:, 1536:2048]
    gq_ref[0] = p[:, 2048:2304] * (GLA_DK ** -0.5)
    gk_ref[0] = p[:, 2304:2560]
    gv_ref[0] = p[:, 2560:3072]
    gg_ref[0] = p[:, 3072:3584]
    lr = jnp.dot(h, wl_ref[...], precision=HIGHEST, preferred_element_type=F32)
    pre = jnp.dot(lr, up_ref[...], precision=HIGHEST, preferred_element_type=F32) + gb_ref[...]
    la = _log_sigmoid(pre) * (1.0 / GLA_GATE_NORM)
    laf_ref[0] = la[:, :GLA_KW]
    lab_ref[0] = la[:, GLA_KW:]


def _inproj(x, shift, scale, norm_w, w_main, w_lr, gk_up, gk_bias, cos, sin, tm):
    b, l, d = x.shape
    nt = l // tm
    tok = lambda w: pl.BlockSpec((1, tm, w), lambda bi, i: (bi, i, 0))
    full = lambda shp: pl.BlockSpec(shp, lambda bi, i: tuple(0 for _ in shp))
    vec = pl.BlockSpec((1, 1, d), lambda bi, i: (bi, 0, 0))
    widths = (RET_WIDTH, RET_WIDTH, RET_WIDTH, RET_WIDTH, GLA_KW, GLA_KW, GLA_WIDTH, GLA_WIDTH, GLA_KW, GLA_KW)
    return pl.pallas_call(
        _inproj_kernel,
        grid=(b, nt),
        in_specs=[tok(d), vec, vec, full((1, d)), full((d, MAIN_COLS)), full((d, LANES)),
                  full((LANES, 2 * GLA_KW)), full((1, 2 * GLA_KW)),
                  pl.BlockSpec((tm, RET_DK), lambda bi, i: (i, 0)),
                  pl.BlockSpec((tm, RET_DK), lambda bi, i: (i, 0))],
        out_specs=[tok(w) for w in widths],
        out_shape=[jax.ShapeDtypeStruct((b, l, w), F32) for w in widths],
        compiler_params=_cparams(("arbitrary", "arbitrary")),
        name="inproj",
    )(x, shift, scale, norm_w, w_main, w_lr, gk_up, gk_bias, cos, sin)


def _scan_kernel(decf_ref, decb_ref,
                 rqf_ref, rkf_ref, rvf_ref, rqb_ref, rkb_ref, rvb_ref,
                 gqf_ref, gkf_ref, gvf_ref, laf_ref, gqb_ref, gkb_ref, gvb_ref, lab_ref,
                 s0rf_ref, s0rb_ref, s0gf_ref, s0gb_ref,
                 orf_ref, orb_ref, ogf_ref, ogb_ref,
                 srf_out, srb_out, sgf_out, sgb_out,
                 srf, srb, sgf, sgb, *, nch):
    i = pl.program_id(1)

    @pl.when(i == 0)
    def _():
        srf[...] = s0rf_ref[0]
        srb[...] = s0rb_ref[0]
        sgf[...] = s0gf_ref[0]
        sgb[...] = s0gb_ref[0]

    C = CHUNK
    row = lax.broadcasted_iota(jnp.int32, (C, C), 0)
    col = lax.broadcasted_iota(jnp.int32, (C, C), 1)
    rel_f = (row - col).astype(F32)
    rel_b = (col - row).astype(F32)
    causal = row >= col
    anti = col >= row
    tri_lo = jnp.where(causal, 1.0, 0.0).astype(F32)
    tri_up = jnp.where(anti, 1.0, 0.0).astype(F32)
    pos_col = lax.broadcasted_iota(jnp.int32, (C, LANES), 0).astype(F32)

    lgf = _log_sigmoid(decf_ref[...])
    lgb = _log_sigmoid(decb_ref[...])

    def ret_consts(lg, h, fwd):
        lrow = lg[h:h + 1, :]
        rel = rel_f if fwd else rel_b
        mask = jnp.where(rel >= 0, jnp.exp(lrow[:, :C] * jnp.maximum(rel, 0.0)), 0.0)
        if fwd:
            qd = jnp.exp(lrow * (pos_col + 1.0))
            kd = jnp.exp(lrow * (C - 1.0 - pos_col))
        else:
            qd = jnp.exp(lrow * (C - pos_col))
            kd = jnp.exp(lrow * pos_col)
        cd = jnp.exp(lrow * float(C))
        return mask, qd, kd, cd

    cf = [ret_consts(lgf, h, True) for h in range(RET_HEADS)]
    cb = [ret_consts(lgb, h, False) for h in range(RET_HEADS)]

    def ret_chunk(q_ref, k_ref, v_ref, o_ref, s_ref, consts, r0):
        for h in range(RET_HEADS):
            mask, qd, kd, cd = consts[h]
            lo = h * RET_DK
            q = q_ref[0, r0:r0 + C, lo:lo + RET_DK]
            k = k_ref[0, r0:r0 + C, lo:lo + RET_DK]
            v = v_ref[0, r0:r0 + C, lo:lo + RET_DK].astype(BF16)
            s = s_ref[h]
            scores = _dot_nt(q.astype(BF16), k.astype(BF16)) * mask
            o = _dot(scores.astype(BF16), v) + _dot((q * qd).astype(BF16), s.astype(BF16))
            o_ref[0, r0:r0 + C, lo:lo + RET_DK] = o
            s_ref[h] = cd * s + _dot((k * kd).T.astype(BF16), v)

    def gla_chunk(q_ref, k_ref, v_ref, a_ref, o_ref, s_ref, fwd, r0):
        a = a_ref[0, r0:r0 + C, :]
        tri = tri_lo if fwd else tri_up
        g = jnp.dot(tri, a, precision=HIGHEST, preferred_element_type=F32)
        g_last = g[C - 1:C, :] if fwd else g[0:1, :]
        eg = jnp.exp(g)
        qt = q_ref[0, r0:r0 + C, :] * eg
        kk = k_ref[0, r0:r0 + C, :]
        kt = kk * jnp.exp(-g)
        kl = kk * jnp.exp(g_last - g)
        tot_col = jnp.exp(jnp.sum(a.T, axis=1, keepdims=True))
        keep = causal if fwd else anti
        for h in range(GLA_HEADS):
            lo = h * GLA_DK
            vo = h * GLA_DV
            v = v_ref[0, r0:r0 + C, vo:vo + GLA_DV].astype(BF16)
            s = s_ref[h]
            qh = qt[:, lo:lo + GLA_DK].astype(BF16)
            scores = jnp.where(keep, _dot_nt(qh, kt[:, lo:lo + GLA_DK].astype(BF16)), 0.0)
            o = _dot(scores.astype(BF16), v) + _dot(qh, s.astype(BF16))
            o_ref[0, r0:r0 + C, vo:vo + GLA_DV] = o
            s_ref[h] = tot_col[lo:lo + GLA_DK, :] * s + _dot(kl[:, lo:lo + GLA_DK].T.astype(BF16), v)

    for c in range(nch):
        rf = c * C
        rb = (nch - 1 - c) * C
        ret_chunk(rqf_ref, rkf_ref, rvf_ref, orf_ref, srf, cf, rf)
        ret_chunk(rqb_ref, rkb_ref, rvb_ref, orb_ref, srb, cb, rb)
        gla_chunk(gqf_ref, gkf_ref, gvf_ref, laf_ref, ogf_ref, sgf, True, rf)
        gla_chunk(gqb_ref, gkb_ref, gvb_ref, lab_ref, ogb_ref, sgb, False, rb)

    @pl.when(i == pl.num_programs(1) - 1)
    def _():
        srf_out[0] = srf[...]
        srb_out[0] = srb[...]
        sgf_out[0] = sgf[...]
        sgb_out[0] = sgb[...]


def _scan(feats, dec_f, dec_b, states, tb):
    rq, rk, rv, gq, gk, gv, la_f, la_b = feats
    b, l, _ = rq.shape
    nb = l // tb
    fwd = lambda w: pl.BlockSpec((1, tb, w), lambda bi, i: (bi, i, 0))
    bwd = lambda w: pl.BlockSpec((1, tb, w), lambda bi, i: (bi, nb - 1 - i, 0))
    dec = pl.BlockSpec((8, LANES), lambda bi, i: (0, 0))
    st_r = pl.BlockSpec((1, RET_HEADS, RET_DK, RET_DK), lambda bi, i: (bi, 0, 0, 0))
    st_g = pl.BlockSpec((1, GLA_HEADS, GLA_DK, GLA_DV), lambda bi, i: (bi, 0, 0, 0))
    o_shape = jax.ShapeDtypeStruct((b, l, RET_WIDTH), F32)
    sr_shape = jax.ShapeDtypeStruct((b, RET_HEADS, RET_DK, RET_DK), F32)
    sg_shape = jax.ShapeDtypeStruct((b, GLA_HEADS, GLA_DK, GLA_DV), F32)
    return pl.pallas_call(
        functools.partial(_scan_kernel, nch=tb // CHUNK),
        grid=(b, nb),
        in_specs=[dec, dec,
                  fwd(RET_WIDTH), fwd(RET_WIDTH), fwd(RET_WIDTH),
                  bwd(RET_WIDTH), bwd(RET_WIDTH), bwd(RET_WIDTH),
                  fwd(GLA_KW), fwd(GLA_KW), fwd(GLA_WIDTH), fwd(GLA_KW),
                  bwd(GLA_KW), bwd(GLA_KW), bwd(GLA_WIDTH), bwd(GLA_KW),
                  st_r, st_r, st_g, st_g],
        out_specs=[fwd(RET_WIDTH), bwd(RET_WIDTH), fwd(GLA_WIDTH), bwd(GLA_WIDTH),
                   st_r, st_r, st_g, st_g],
        out_shape=[o_shape, o_shape, o_shape, o_shape, sr_shape, sr_shape, sg_shape, sg_shape],
        scratch_shapes=[pltpu.VMEM((RET_HEADS, RET_DK, RET_DK), F32),
                        pltpu.VMEM((RET_HEADS, RET_DK, RET_DK), F32),
                        pltpu.VMEM((GLA_HEADS, GLA_DK, GLA_DV), F32),
                        pltpu.VMEM((GLA_HEADS, GLA_DK, GLA_DV), F32)],
        compiler_params=_cparams(("arbitrary", "arbitrary")),
        name="bidir_scan",
    )(dec_f, dec_b, rq, rk, rv, rq, rk, rv, gq, gk, gv, la_f, gq, gk, gv, la_b, *states)


def _mix_kernel(orf_ref, orb_ref, ogf_ref, ogb_ref, rg_ref, gg_ref, x_ref, g1_ref, sh_ref, sc_ref,
                nw_ref, wo_ref, x1_ref, h2t_ref):
    def head_norm(o):
        parts = []
        for h in range(RET_HEADS):
            oh = o[:, h * 128:(h + 1) * 128]
            parts.append(oh * lax.rsqrt(jnp.mean(oh * oh, axis=-1, keepdims=True) + EPS))
        return jnp.concatenate(parts, axis=-1)

    ret = head_norm(orf_ref[0] + orb_ref[0]) * _silu(rg_ref[0])
    gla = head_norm(ogf_ref[0] + ogb_ref[0]) * _silu(gg_ref[0])
    y = jnp.concatenate([ret, gla], axis=-1).astype(BF16)
    x1 = x_ref[0] + g1_ref[0] * _dot(y, wo_ref[...])
    x1_ref[0] = x1
    ms = jnp.mean(x1 * x1, axis=-1, keepdims=True)
    h2 = x1 * lax.rsqrt(ms + EPS) * nw_ref[...]
    h2 = h2 * (1.0 + sc_ref[0]) + sh_ref[0]
    h2t_ref[...] = h2.T.astype(BF16)


def _mix(o_rf, o_rb, o_gf, o_gb, rg, gg, x, g1, sh2, sc2, norm_w, w_out, tm):
    b, l, d = x.shape
    nt = l // tm
    tok = lambda w: pl.BlockSpec((1, tm, w), lambda bi, i: (bi, i, 0))
    vec = pl.BlockSpec((1, 1, d), lambda bi, i: (bi, 0, 0))
    return pl.pallas_call(
        _mix_kernel,
        grid=(b, nt),
        in_specs=[tok(512)] * 6 + [tok(d), vec, vec, vec,
                                   pl.BlockSpec((1, d), lambda bi, i: (0, 0)),
                                   pl.BlockSpec((d, d), lambda bi, i: (0, 0))],
        out_specs=[tok(d), pl.BlockSpec((d, tm), lambda bi, i: (0, bi * nt + i))],
        out_shape=[jax.ShapeDtypeStruct((b, l, d), F32), jax.ShapeDtypeStruct((d, b * l), BF16)],
        compiler_params=_cparams(("arbitrary", "arbitrary")),
        name="mix_outproj",
    )(o_rf, o_rb, o_gf, o_gb, rg, gg, x, g1, sh2, sc2, norm_w, w_out)


def _peer_kernel(h2t_ref, wqt_ref, k1_ref, k2_ref, down_ref, upt_ref, x1_ref, g2_ref, nf_ref,
                 out_ref,
                 rank2_s, cnt_s, e1_s, e2_s, v1_s, v2_s, acc_s, *, ib):
    e = pl.program_id(1)
    tm = h2t_ref.shape[1]
    neg = -jnp.inf

    def top16(s, v_s, want_rank):
        rank = jnp.full(s.shape, float(PEER_TOPK), F32)
        for r in range(PEER_TOPK):
            m = jnp.max(s, axis=0, keepdims=True)
            v_s[r:r + 1, :] = m
            hit = s == m
            if want_rank:
                rank = jnp.where(hit, float(r), rank)
            s = jnp.where(hit, neg, s)
        return rank

    @pl.when(e == 0)
    def _():
        acc_s[...] = jnp.zeros_like(acc_s)
        ht = h2t_ref[...]
        half = PEER_DQ // 2
        for h in range(PEER_HEADS):
            q = _dot(wqt_ref[h * PEER_DQ:(h + 1) * PEER_DQ, :], ht).astype(BF16)
            s1 = _dot(k1_ref[h], q[:half, :])
            s2 = _dot(k2_ref[h], q[half:, :])
            top16(s1, v1_s, False)
            rank2_s[h] = top16(s2, v2_s, True)
            v1 = v1_s[...]
            v2 = v2_s[...]
            cands = [v1[0:1, :] + v2]
            for a in range(1, 8):
                cands.append(v1[a:a + 1, :] + v2[0:8, :])
            cands.append(v1[8:16, :] + v2[0:1, :])
            cand = jnp.concatenate(cands, axis=0)
            c = cand
            tau = None
            for r in range(PEER_TOPK):
                tau = jnp.max(c, axis=0, keepdims=True)
                c = jnp.where(c == tau, neg, c)
            m1 = v1[0:1, :]
            m2 = v2[0:1, :]
            z = jnp.sum(jnp.where(cand >= tau, jnp.exp(cand - (m1 + m2)), 0.0), axis=0, keepdims=True)
            cnt = jnp.zeros(s1.shape, F32)
            for bb in range(PEER_TOPK):
                cnt = cnt + jnp.where(s1 + v2[bb:bb + 1, :] >= tau, 1.0, 0.0)
            cnt_s[h] = cnt
            e1_s[h] = jnp.exp(s1 - m1) / z
            e2_s[h] = jnp.exp(s2 - m2)

    a_t = _dot(down_ref[...], h2t_ref[...])
    hs = []
    for ii in range(ib):
        i = e * ib + ii
        w = jnp.zeros((N_KEYS, tm), F32)
        for h in range(PEER_HEADS):
            c_row = cnt_s[h, pl.ds(i, 1), :]
            e_row = e1_s[h, pl.ds(i, 1), :]
            w = w + jnp.where(rank2_s[h] < c_row, e2_s[h], 0.0) * e_row
        a = a_t[ii * N_KEYS:(ii + 1) * N_KEYS, :]
        act = 0.5 * a * (1.0 + lax.erf(a * (2.0 ** -0.5)))
        hs.append((w * act).astype(BF16))
    acc_s[...] += _dot(upt_ref[...], jnp.concatenate(hs, axis=0))

    @pl.when(e == pl.num_programs(1) - 1)
    def _():
        y = x1_ref[0] + g2_ref[0] * acc_s[...].T
        ms = jnp.mean(y * y, axis=-1, keepdims=True)
        out_ref[0] = y * lax.rsqrt(ms + EPS) * nf_ref[...]


def _peer(h2t, wq_t, k1, k2, down, up_t, x1, g2, norm_f, tm, ib):
    b, l, d = x1.shape
    nt = l // tm
    ne = N_KEYS // ib
    te = ib * N_KEYS
    const = lambda shp: pl.BlockSpec(shp, lambda t, e: tuple(0 for _ in shp))
    return pl.pallas_call(
        functools.partial(_peer_kernel, ib=ib),
        grid=(b * nt, ne),
        in_specs=[pl.BlockSpec((d, tm), lambda t, e: (0, t)),
                  const((PEER_HEADS * PEER_DQ, d)),
                  const((PEER_HEADS, N_KEYS, PEER_DQ // 2)),
                  const((PEER_HEADS, N_KEYS, PEER_DQ // 2)),
                  pl.BlockSpec((te, d), lambda t, e: (e, 0)),
                  pl.BlockSpec((d, te), lambda t, e: (0, e)),
                  pl.BlockSpec((1, tm, d), lambda t, e: (t // nt, t % nt, 0)),
                  pl.BlockSpec((1, 1, d), lambda t, e: (t // nt, 0, 0)),
                  const((1, d))],
        out_specs=pl.BlockSpec((1, tm, d), lambda t, e: (t // nt, t % nt, 0)),
        out_shape=jax.ShapeDtypeStruct((b, l, d), F32),
        scratch_shapes=[pltpu.VMEM((PEER_HEADS, N_KEYS, tm), F32),
                        pltpu.VMEM((PEER_HEADS, N_KEYS, tm), F32),
                        pltpu.VMEM((PEER_HEADS, N_KEYS, tm), F32),
                        pltpu.VMEM((PEER_HEADS, N_KEYS, tm), F32),
                        pltpu.VMEM((PEER_TOPK, tm), F32),
                        pltpu.VMEM((PEER_TOPK, tm), F32),
                        pltpu.VMEM((d, tm), F32)],
        compiler_params=_cparams(("arbitrary", "arbitrary")),
        name="peer",
    )(h2t, wq_t, k1, k2, down, up_t, x1, g2, norm_f)


def _rope_tables(rows, cols):
    r, c = jnp.meshgrid(jnp.arange(rows, dtype=F32), jnp.arange(cols, dtype=F32), indexing="ij")
    n_freq = RET_DK // 4
    inv = ROPE_BASE ** (-jnp.arange(n_freq, dtype=F32) / n_freq)
    ang = jnp.concatenate([r.reshape(-1, 1) * inv, c.reshape(-1, 1) * inv], axis=-1)
    cos, sin = jnp.cos(ang), jnp.sin(ang)
    return jnp.concatenate([cos, cos], axis=-1), jnp.concatenate([-sin, sin], axis=-1)


def kernel(x, c, ctx, c_ctx, w_mod, b_mod, norm1_w, norm2_w, w_in, ret_decay_f, ret_decay_b,
           gla_gk_up_f, gla_gk_bias_f, gla_gk_up_b, gla_gk_bias_b, w_out, peer_w_q, peer_k1, peer_k2,
           peer_down, peer_up, norm_f_w):
    depth = w_mod.shape[0]
    assert depth == 1
    b, l, d = x.shape
    lc = ctx.shape[1]
    li = 0

    cc = jnp.zeros((8, d), F32).at[:b].set(c).at[b].set(c_ctx)
    mod = _modulation(cc, w_mod[li], b_mod[li][None, :])
    sh1, sc1, g1, sh2, sc2, g2 = [mod[:b, j * d:(j + 1) * d][:, None, :] for j in range(N_MOD)]
    csh1, csc1 = [jnp.broadcast_to(mod[b:b + 1, j * d:(j + 1) * d][:, None, :], (b, 1, d)) for j in range(2)]

    w_main = w_in[li][:, :MAIN_COLS].astype(BF16)
    w_lr = jnp.zeros((d, LANES), F32).at[:, :2 * GLA_RANK].set(w_in[li][:, MAIN_COLS:])
    gk_up = jnp.zeros((LANES, 2 * GLA_KW), F32)
    gk_up = gk_up.at[:GLA_RANK, :GLA_KW].set(gla_gk_up_f[li])
    gk_up = gk_up.at[GLA_RANK:2 * GLA_RANK, GLA_KW:].set(gla_gk_up_b[li])
    gk_bias = jnp.concatenate([gla_gk_bias_f[li], gla_gk_bias_b[li]])[None, :]
    n1 = norm1_w[li][None, :]
    dec_f = jnp.zeros((8, LANES), F32).at[:RET_HEADS].set(jnp.broadcast_to(ret_decay_f[li][:, None], (RET_HEADS, LANES)))
    dec_b = jnp.zeros((8, LANES), F32).at[:RET_HEADS].set(jnp.broadcast_to(ret_decay_b[li][:, None], (RET_HEADS, LANES)))
    cos, sin = _rope_tables(l // GRID_W, GRID_W)
    cos_c = jnp.ones((lc, RET_DK), F32)
    sin_c = jnp.zeros((lc, RET_DK), F32)

    fc = _inproj(ctx, csh1, csc1, n1, w_main, w_lr, gk_up, gk_bias, cos_c, sin_c, tm=256)
    zr = jnp.zeros((b, RET_HEADS, RET_DK, RET_DK), F32)
    zg = jnp.zeros((b, GLA_HEADS, GLA_DK, GLA_DV), F32)
    ctx_out = _scan((fc[0], fc[1], fc[2], fc[4], fc[5], fc[6], fc[8], fc[9]), dec_f, dec_b,
                    (zr, zr, zg, zg), tb=256)
    ctx_states = ctx_out[4:]

    fl = _inproj(x, sh1, sc1, n1, w_main, w_lr, gk_up, gk_bias, cos, sin, tm=256)
    lat = _scan((fl[0], fl[1], fl[2], fl[4], fl[5], fl[6], fl[8], fl[9]), dec_f, dec_b,
                ctx_states, tb=256)

    x1, h2t = _mix(lat[0], lat[1], lat[2], lat[3], fl[3], fl[7], x, g1, sh2, sc2,
                   norm2_w[li][None, :], w_out[li].astype(BF16), tm=256)

    wq_t = peer_w_q[li].T.astype(BF16)
    return _peer(h2t, wq_t, peer_k1[li].astype(BF16), peer_k2[li].astype(BF16),
                 peer_down[li].astype(BF16), peer_up[li].T.astype(BF16), x1, g2, norm_f_w[None, :],
                 tm=512, ib=4)
```

```python
import functools

import jax
import jax.numpy as jnp
from jax import lax
from jax.experimental import pallas as pl
from jax.experimental.pallas import tpu as pltpu

F32 = jnp.float32
BF16 = jnp.bfloat16
HIGHEST = lax.Precision.HIGHEST

D_MODEL = 1024
GRID_W = 64
RET_HEADS = 4
RET_WIDTH = 512
RET_DK = 128
GLA_HEADS = 4
GLA_WIDTH = 512
GLA_DV = 128
GLA_DK = 64
GLA_KW = GLA_HEADS * GLA_DK
GLA_RANK = 16
GLA_GATE_NORM = 16.0
CHUNK = 64
ROPE_BASE = 10000.0
N_KEYS = 128
PEER_HEADS = 8
PEER_TOPK = 16
PEER_DQ = 256
N_MOD = 6
EPS = 1e-6
MAIN_COLS = 3584
LANES = 128
SUBLANES = 8
PACK16 = 16
VMEM_LIMIT = 56 * 1024 * 1024


def _cparams(sem):
    return pltpu.CompilerParams(dimension_semantics=sem, vmem_limit_bytes=VMEM_LIMIT)


def _silu(v):
    return v / (1.0 + jnp.exp(-v))


def _log_sigmoid(z):
    return jnp.minimum(z, 0.0) - jnp.log(1.0 + jnp.exp(-jnp.abs(z)))


def _dot(a, b):
    return jnp.dot(a, b, preferred_element_type=F32)


def _dot_nt(a, b):
    return lax.dot_general(a, b, (((1,), (1,)), ((), ())), preferred_element_type=F32)


def _mod_kernel(c_ref, w_ref, b_ref, o_ref):
    a = _silu(c_ref[...])
    o_ref[...] = jnp.dot(a, w_ref[...], precision=HIGHEST, preferred_element_type=F32) + b_ref[...]


def _modulation(cc, w_mod, b_mod):
    n = w_mod.shape[1]
    tn = 1536
    return pl.pallas_call(
        _mod_kernel,
        grid=(n // tn,),
        in_specs=[pl.BlockSpec((8, D_MODEL), lambda j: (0, 0)),
                  pl.BlockSpec((D_MODEL, tn), lambda j: (0, j)),
                  pl.BlockSpec((1, tn), lambda j: (0, j))],
        out_specs=pl.BlockSpec((8, tn), lambda j: (0, j)),
        out_shape=jax.ShapeDtypeStruct((8, n), F32),
        compiler_params=_cparams(("arbitrary",)),
        name="modulation",
    )(cc, w_mod, b_mod)


def _inproj_kernel(x_ref, sh_ref, sc_ref, nw_ref, wm_ref, wl_ref, up_ref, gb_ref, cos_ref, sin_ref,
                   rq_ref, rk_ref, rv_ref, rg_ref, gq_ref, gk_ref, gv_ref, gg_ref, laf_ref, lab_ref):
    x = x_ref[0]
    ms = jnp.mean(x * x, axis=-1, keepdims=True)
    h = x * lax.rsqrt(ms + EPS) * nw_ref[...]
    h = h * (1.0 + sc_ref[0]) + sh_ref[0]
    p = _dot(h.astype(BF16), wm_ref[...])
    cos = cos_ref[...]
    sin = sin_ref[...]
    k_scale = RET_DK ** -0.5
    for hh in range(RET_HEADS):
        lo = hh * RET_DK
        q = p[:, lo:lo + RET_DK]
        k = p[:, RET_WIDTH + lo:RET_WIDTH + lo + RET_DK] * k_scale
        rq_ref[0, :, lo:lo + RET_DK] = q * cos + pltpu.roll(q, RET_DK // 2, axis=1) * sin
        rk_ref[0, :, lo:lo + RET_DK] = k * cos + pltpu.roll(k, RET_DK // 2, axis=1) * sin
    rv_ref[0] = p[:, 1024:1536]
    rg_ref[0] = p[:, 1536:2048]
    gq_ref[0] = p[:, 2048:2304] * (GLA_DK ** -0.5)
    gk_ref[0] = p[:, 2304:2560]
    gv_ref[0] = p[:, 2560:3072]
    gg_ref[0] = p[:, 3072:3584]
    lr = jnp.dot(h, wl_ref[...], precision=HIGHEST, preferred_element_type=F32)
    pre = jnp.dot(lr, up_ref[...], precision=HIGHEST, preferred_element_type=F32) + gb_ref[...]
    la = _log_sigmoid(pre) * (1.0 / GLA_GATE_NORM)
    laf_ref[0] = la[:, :GLA_KW]
    lab_ref[0] = la[:, GLA_KW:]


def _inproj(x, shift, scale, norm_w, w_main, w_lr, gk_up, gk_bias, cos, sin, tm):
    b, l, d = x.shape
    nt = l // tm
    tok = lambda w: pl.BlockSpec((1, tm, w), lambda bi, i: (bi, i, 0))
    full = lambda shp: pl.BlockSpec(shp, lambda bi, i: tuple(0 for _ in shp))
    vec = pl.BlockSpec((1, 1, d), lambda bi, i: (bi, 0, 0))
    widths = (RET_WIDTH, RET_WIDTH, RET_WIDTH, RET_WIDTH, GLA_KW, GLA_KW, GLA_WIDTH, GLA_WIDTH, GLA_KW, GLA_KW)
    return pl.pallas_call(
        _inproj_kernel,
        grid=(b, nt),
        in_specs=[tok(d), vec, vec, full((1, d)), full((d, MAIN_COLS)), full((d, LANES)),
                  full((LANES, 2 * GLA_KW)), full((1, 2 * GLA_KW)),
                  pl.BlockSpec((tm, RET_DK), lambda bi, i: (i, 0)),
                  pl.BlockSpec((tm, RET_DK), lambda bi, i: (i, 0))],
        out_specs=[tok(w) for w in widths],
        out_shape=[jax.ShapeDtypeStruct((b, l, w), F32) for w in widths],
        compiler_params=_cparams(("arbitrary", "arbitrary")),
        name="inproj",
    )(x, shift, scale, norm_w, w_main, w_lr, gk_up, gk_bias, cos, sin)


def _scan_kernel(decf_ref, decb_ref,
                 rqf_ref, rkf_ref, rvf_ref, rqb_ref, rkb_ref, rvb_ref,
                 gqf_ref, gkf_ref, gvf_ref, laf_ref, gqb_ref, gkb_ref, gvb_ref, lab_ref,
                 s0rf_ref, s0rb_ref, s0gf_ref, s0gb_ref,
                 orf_ref, orb_ref, ogf_ref, ogb_ref,
                 srf_out, srb_out, sgf_out, sgb_out,
                 srf, srb, sgf, sgb, *, nch):
    i = pl.program_id(1)

    @pl.when(i == 0)
    def _():
        srf[...] = s0rf_ref[0]
        srb[...] = s0rb_ref[0]
        sgf[...] = s0gf_ref[0]
        sgb[...] = s0gb_ref[0]

    C = CHUNK
    row = lax.broadcasted_iota(jnp.int32, (C, C), 0)
    col = lax.broadcasted_iota(jnp.int32, (C, C), 1)
    rel_f = (row - col).astype(F32)
    rel_b = (col - row).astype(F32)
    causal = row >= col
    anti = col >= row
    tri_lo = jnp.where(causal, 1.0, 0.0).astype(F32)
    tri_up = jnp.where(anti, 1.0, 0.0).astype(F32)
    pos_col = lax.broadcasted_iota(jnp.int32, (C, LANES), 0).astype(F32)

    lgf = _log_sigmoid(decf_ref[...])
    lgb = _log_sigmoid(decb_ref[...])

    def ret_consts(lg, h, fwd):
        lrow = lg[h:h + 1, :]
        rel = rel_f if fwd else rel_b
        mask = jnp.where(rel >= 0, jnp.exp(lrow[:, :C] * jnp.maximum(rel, 0.0)), 0.0)
        if fwd:
            qd = jnp.exp(lrow * (pos_col + 1.0))
            kd = jnp.exp(lrow * (C - 1.0 - pos_col))
        else:
            qd = jnp.exp(lrow * (C - pos_col))
            kd = jnp.exp(lrow * pos_col)
        cd = jnp.exp(lrow * float(C))
        return mask, qd, kd, cd

    cf = [ret_consts(lgf, h, True) for h in range(RET_HEADS)]
    cb = [ret_consts(lgb, h, False) for h in range(RET_HEADS)]

    def ret_chunk(q_ref, k_ref, v_ref, o_ref, s_ref, consts, r0):
        for h in range(RET_HEADS):
            mask, qd, kd, cd = consts[h]
            lo = h * RET_DK
            q = q_ref[0, r0:r0 + C, lo:lo + RET_DK]
            k = k_ref[0, r0:r0 + C, lo:lo + RET_DK]
            v = v_ref[0, r0:r0 + C, lo:lo + RET_DK].astype(BF16)
            s = s_ref[h]
            scores = _dot_nt(q.astype(BF16), k.astype(BF16)) * mask
            o = _dot(scores.astype(BF16), v) + _dot((q * qd).astype(BF16), s.astype(BF16))
            o_ref[0, r0:r0 + C, lo:lo + RET_DK] = o
            s_ref[h] = cd * s + _dot((k * kd).T.astype(BF16), v)

    def gla_chunk(q_ref, k_ref, v_ref, a_ref, o_ref, s_ref, fwd, r0):
        a = a_ref[0, r0:r0 + C, :]
        tri = tri_lo if fwd else tri_up
        g = jnp.dot(tri, a, precision=HIGHEST, preferred_element_type=F32)
        g_last = g[C - 1:C, :] if fwd else g[0:1, :]
        eg = jnp.exp(g)
        qt = q_ref[0, r0:r0 + C, :] * eg
        kk = k_ref[0, r0:r0 + C, :]
        kt = kk * jnp.exp(-g)
        kl = kk * jnp.exp(g_last - g)
        tot_col = jnp.exp(jnp.sum(a.T, axis=1, keepdims=True))
        keep = causal if fwd else anti
        for h in range(GLA_HEADS):
            lo = h * GLA_DK
            vo = h * GLA_DV
            v = v_ref[0, r0:r0 + C, vo:vo + GLA_DV].astype(BF16)
            s = s_ref[h]
            qh = qt[:, lo:lo + GLA_DK].astype(BF16)
            scores = jnp.where(keep, _dot_nt(qh, kt[:, lo:lo + GLA_DK].astype(BF16)), 0.0)
            o = _dot(scores.astype(BF16), v) + _dot(qh, s.astype(BF16))
            o_ref[0, r0:r0 + C, vo:vo + GLA_DV] = o
            s_ref[h] = tot_col[lo:lo + GLA_DK, :] * s + _dot(kl[:, lo:lo + GLA_DK].T.astype(BF16), v)

    for c in range(nch):
        rf = c * C
        rb = (nch - 1 - c) * C
        ret_chunk(rqf_ref, rkf_ref, rvf_ref, orf_ref, srf, cf, rf)
        ret_chunk(rqb_ref, rkb_ref, rvb_ref, orb_ref, srb, cb, rb)
        gla_chunk(gqf_ref, gkf_ref, gvf_ref, laf_ref, ogf_ref, sgf, True, rf)
        gla_chunk(gqb_ref, gkb_ref, gvb_ref, lab_ref, ogb_ref, sgb, False, rb)

    @pl.when(i == pl.num_programs(1) - 1)
    def _():
        srf_out[0] = srf[...]
        srb_out[0] = srb[...]
        sgf_out[0] = sgf[...]
        sgb_out[0] = sgb[...]


def _scan(feats, dec_f, dec_b, states, tb):
    rq, rk, rv, gq, gk, gv, la_f, la_b = feats
    b, l, _ = rq.shape
    nb = l // tb
    fwd = lambda w: pl.BlockSpec((1, tb, w), lambda bi, i: (bi, i, 0))
    bwd = lambda w: pl.BlockSpec((1, tb, w), lambda bi, i: (bi, nb - 1 - i, 0))
    dec = pl.BlockSpec((8, LANES), lambda bi, i: (0, 0))
    st_r = pl.BlockSpec((1, RET_HEADS, RET_DK, RET_DK), lambda bi, i: (bi, 0, 0, 0))
    st_g = pl.BlockSpec((1, GLA_HEADS, GLA_DK, GLA_DV), lambda bi, i: (bi, 0, 0, 0))
    o_shape = jax.ShapeDtypeStruct((b, l, RET_WIDTH), F32)
    sr_shape = jax.ShapeDtypeStruct((b, RET_HEADS, RET_DK, RET_DK), F32)
    sg_shape = jax.ShapeDtypeStruct((b, GLA_HEADS, GLA_DK, GLA_DV), F32)
    return pl.pallas_call(
        functools.partial(_scan_kernel, nch=tb // CHUNK),
        grid=(b, nb),
        in_specs=[dec, dec,
                  fwd(RET_WIDTH), fwd(RET_WIDTH), fwd(RET_WIDTH),
                  bwd(RET_WIDTH), bwd(RET_WIDTH), bwd(RET_WIDTH),
                  fwd(GLA_KW), fwd(GLA_KW), fwd(GLA_WIDTH), fwd(GLA_KW),
                  bwd(GLA_KW), bwd(GLA_KW), bwd(GLA_WIDTH), bwd(GLA_KW),
                  st_r, st_r, st_g, st_g],
        out_specs=[fwd(RET_WIDTH), bwd(RET_WIDTH), fwd(GLA_WIDTH), bwd(GLA_WIDTH),
                   st_r, st_r, st_g, st_g],
        out_shape=[o_shape, o_shape, o_shape, o_shape, sr_shape, sr_shape, sg_shape, sg_shape],
        scratch_shapes=[pltpu.VMEM((RET_HEADS, RET_DK, RET_DK), F32),
                        pltpu.VMEM((RET_HEADS, RET_DK, RET_DK), F32),
                        pltpu.VMEM((GLA_HEADS, GLA_DK, GLA_DV), F32),
                        pltpu.VMEM((GLA_HEADS, GLA_DK, GLA_DV), F32)],
        compiler_params=_cparams(("arbitrary", "arbitrary")),
        name="bidir_scan",
    )(dec_f, dec_b, rq, rk, rv, rq, rk, rv, gq, gk, gv, la_f, gq, gk, gv, la_b, *states)


def _mix_kernel(orf_ref, orb_ref, ogf_ref, ogb_ref, rg_ref, gg_ref, x_ref, g1_ref, sh_ref, sc_ref,
                nw_ref, wo_ref, x1_ref, h2t_ref):
    def head_norm(o):
        parts = []
        for h in range(RET_HEADS):
            oh = o[:, h * 128:(h + 1) * 128]
            parts.append(oh * lax.rsqrt(jnp.mean(oh * oh, axis=-1, keepdims=True) + EPS))
        return jnp.concatenate(parts, axis=-1)

    ret = head_norm(orf_ref[0] + orb_ref[0]) * _silu(rg_ref[0])
    gla = head_norm(ogf_ref[0] + ogb_ref[0]) * _silu(gg_ref[0])
    y = jnp.concatenate([ret, gla], axis=-1).astype(BF16)
    x1 = x_ref[0] + g1_ref[0] * _dot(y, wo_ref[...])
    x1_ref[0] = x1
    ms = jnp.mean(x1 * x1, axis=-1, keepdims=True)
    h2 = x1 * lax.rsqrt(ms + EPS) * nw_ref[...]
    h2 = h2 * (1.0 + sc_ref[0]) + sh_ref[0]
    h2t_ref[...] = h2.T.astype(BF16)


def _mix(o_rf, o_rb, o_gf, o_gb, rg, gg, x, g1, sh2, sc2, norm_w, w_out, tm):
    b, l, d = x.shape
    nt = l // tm
    tok = lambda w: pl.BlockSpec((1, tm, w), lambda bi, i: (bi, i, 0))
    vec = pl.BlockSpec((1, 1, d), lambda bi, i: (bi, 0, 0))
    return pl.pallas_call(
        _mix_kernel,
        grid=(b, nt),
        in_specs=[tok(512)] * 6 + [tok(d), vec, vec, vec,
                                   pl.BlockSpec((1, d), lambda bi, i: (0, 0)),
                                   pl.BlockSpec((d, d), lambda bi, i: (0, 0))],
        out_specs=[tok(d), pl.BlockSpec((d, tm), lambda bi, i: (0, bi * nt + i))],
        out_shape=[jax.ShapeDtypeStruct((b, l, d), F32), jax.ShapeDtypeStruct((d, b * l), BF16)],
        compiler_params=_cparams(("arbitrary", "arbitrary")),
        name="mix_outproj",
    )(o_rf, o_rb, o_gf, o_gb, rg, gg, x, g1, sh2, sc2, norm_w, w_out)


def _oddeven_merge_sort_pairs(n):
    pairs = []
    p = 1
    while p < n:
        k = p
        while k >= 1:
            for j in range(k % p, n - k, 2 * k):
                for i in range(min(k, n - j - k)):
                    if (i + j) // (2 * p) == (i + j + k) // (2 * p):
                        pairs.append((i + j, i + j + k))
            k //= 2
        p *= 2
    return pairs


def _bitonic_merge_pairs(n):
    pairs = []
    d = n // 2
    while d >= 1:
        pairs.extend((i, i + d) for i in range(n) if (i & d) == 0)
        d //= 2
    return pairs


_SORT16 = _oddeven_merge_sort_pairs(PEER_TOPK)
_MERGE16 = _bitonic_merge_pairs(PEER_TOPK)


def _compare_exchange(x, pairs):
    for a, b in pairs:
        hi = jnp.maximum(x[a], x[b])
        lo = jnp.minimum(x[a], x[b])
        x[a] = hi
        x[b] = lo


def _sublane_allreduce(v, op):
    for sh in (4, 2, 1):
        v = op(v, pltpu.roll(v, sh, axis=0))
    return v


def _top16_desc(s):
    x = [s[SUBLANES * k:SUBLANES * (k + 1), :] for k in range(PEER_TOPK)]
    _compare_exchange(x, _SORT16)
    for sh in (4, 2, 1):
        c = [jnp.maximum(x[k], pltpu.roll(x[PEER_TOPK - 1 - k], sh, axis=0)) for k in range(PEER_TOPK)]
        _compare_exchange(c, _MERGE16)
        x = c
    return x


def _route_group(s1, s2):
    inf = jnp.inf
    v1 = _top16_desc(s1)
    v2 = _top16_desc(s2)
    sub = lax.broadcasted_iota(jnp.int32, v1[0].shape, 0)

    def pack_rows(v):
        out = v[0]
        for r in range(1, SUBLANES):
            out = jnp.where(sub == r, v[r], out)
        return out

    v1lo, v1hi = pack_rows(v1[:SUBLANES]), pack_rows(v1[SUBLANES:])
    v2lo, v2hi = pack_rows(v2[:SUBLANES]), pack_rows(v2[SUBLANES:])
    cands = [v1[0] + v2lo, v1[0] + v2hi] + [v1[a] + v2lo for a in range(1, SUBLANES)] + [v1hi + v2[0]]
    c = list(cands)
    tau = None
    for _ in range(PEER_TOPK):
        m = c[0]
        for ci in c[1:]:
            m = jnp.maximum(m, ci)
        tau = _sublane_allreduce(m, jnp.maximum)
        c = [jnp.where(ci == tau, -inf, ci) for ci in c]
    m12 = v1[0] + v2[0]
    zs = None
    for cd in cands:
        t = jnp.where(cd >= tau, jnp.exp(cd - m12), 0.0)
        zs = t if zs is None else zs + t
    zinv = 1.0 / _sublane_allreduce(zs, jnp.add)
    thetas = []
    for b in range(PEER_TOPK):
        lo = jnp.where(v1lo + v2[b] >= tau, v1lo, inf)
        hi = jnp.where(v1hi + v2[b] >= tau, v1hi, inf)
        thetas.append(_sublane_allreduce(jnp.minimum(lo, hi), jnp.minimum))
    cnt, e1, rank, e2 = [], [], [], []
    for k in range(N_KEYS // SUBLANES):
        x = s1[SUBLANES * k:SUBLANES * (k + 1), :]
        y = s2[SUBLANES * k:SUBLANES * (k + 1), :]
        ck = jnp.where(x >= thetas[0], 1.0, 0.0)
        rk = jnp.where(v2[0] > y, 1.0, 0.0)
        for b in range(1, PEER_TOPK):
            ck = ck + jnp.where(x >= thetas[b], 1.0, 0.0)
            rk = rk + jnp.where(v2[b] > y, 1.0, 0.0)
        cnt.append(ck)
        rank.append(rk)
        e1.append(jnp.exp(x - v1[0]) * zinv)
        e2.append(jnp.exp(y - v2[0]))
    return cnt, e1, rank, e2


def _peer_kernel(h2t_ref, wqt_ref, k1_ref, k2_ref, down_ref, upt_ref, x1_ref, g2_ref, nf_ref,
                 out_ref,
                 rank2_s, e2_s, cnt_s, e1_s, acc_s, *, ib):
    e = pl.program_id(1)
    tm = h2t_ref.shape[1]
    half = PEER_DQ // 2
    nk2 = N_KEYS // PACK16

    @pl.when(e == 0)
    def _():
        acc_s[...] = jnp.zeros_like(acc_s)

        def head_body(h, carry):
            ht = h2t_ref[...]
            row0 = pl.multiple_of(h * PEER_DQ, PEER_DQ)
            q = _dot(wqt_ref[pl.ds(row0, PEER_DQ), :], ht).astype(BF16)
            s1 = _dot(k1_ref[h], q[:half, :])
            s2 = _dot(k2_ref[h], q[half:, :])
            for g in range(tm // LANES):
                sl = slice(g * LANES, (g + 1) * LANES)
                cnt, e1, rank, e2 = _route_group(s1[:, sl], s2[:, sl])
                cnt_s[h, :, sl] = jnp.concatenate(cnt, axis=0)
                e1_s[h, :, sl] = jnp.concatenate(e1, axis=0)
                for k2 in range(nk2):
                    rank2_s[h, k2, :, sl] = jnp.concatenate(rank[2 * k2:2 * k2 + 2], axis=0).astype(BF16)
                    e2_s[h, k2, :, sl] = jnp.concatenate(e2[2 * k2:2 * k2 + 2], axis=0).astype(BF16)
            return carry

        lax.fori_loop(0, PEER_HEADS, head_body, 0)

    a_t = _dot(down_ref[...], h2t_ref[...])
    hs = []
    for ii in range(ib):
        i = e * ib + ii
        w = jnp.zeros((nk2, PACK16, tm), BF16)
        for h in range(PEER_HEADS):
            c_row = jnp.broadcast_to(cnt_s[h, pl.ds(i, 1), :], (PACK16, tm)).astype(BF16)
            e_row = jnp.broadcast_to(e1_s[h, pl.ds(i, 1), :], (PACK16, tm)).astype(BF16)
            w = w + jnp.where(rank2_s[h] < c_row[None], e2_s[h], jnp.zeros((), BF16)) * e_row[None]
        a = a_t[ii * N_KEYS:(ii + 1) * N_KEYS, :]
        act = 0.5 * a * (1.0 + lax.erf(a * (2.0 ** -0.5)))
        hs.append((w * act.astype(BF16).reshape(nk2, PACK16, tm)).reshape(N_KEYS, tm))
    acc_s[...] += _dot(upt_ref[...], jnp.concatenate(hs, axis=0))

    @pl.when(e == pl.num_programs(1) - 1)
    def _():
        y = x1_ref[0] + g2_ref[0] * acc_s[...].T
        ms = jnp.mean(y * y, axis=-1, keepdims=True)
        out_ref[0] = y * lax.rsqrt(ms + EPS) * nf_ref[...]


def _peer(h2t, wq_t, k1, k2, down, up_t, x1, g2, norm_f, tm, ib):
    b, l, d = x1.shape
    nt = l // tm
    ne = N_KEYS // ib
    te = ib * N_KEYS
    const = lambda shp: pl.BlockSpec(shp, lambda t, e: tuple(0 for _ in shp))
    return pl.pallas_call(
        functools.partial(_peer_kernel, ib=ib),
        grid=(b * nt, ne),
        in_specs=[pl.BlockSpec((d, tm), lambda t, e: (0, t)),
                  const((PEER_HEADS * PEER_DQ, d)),
                  const((PEER_HEADS, N_KEYS, PEER_DQ // 2)),
                  const((PEER_HEADS, N_KEYS, PEER_DQ // 2)),
                  pl.BlockSpec((te, d), lambda t, e: (e, 0)),
                  pl.BlockSpec((d, te), lambda t, e: (0, e)),
                  pl.BlockSpec((1, tm, d), lambda t, e: (t // nt, t % nt, 0)),
                  pl.BlockSpec((1, 1, d), lambda t, e: (t // nt, 0, 0)),
                  const((1, d))],
        out_specs=pl.BlockSpec((1, tm, d), lambda t, e: (t // nt, t % nt, 0)),
        out_shape=jax.ShapeDtypeStruct((b, l, d), F32),
        scratch_shapes=[pltpu.VMEM((PEER_HEADS, N_KEYS // PACK16, PACK16, tm), BF16),
                        pltpu.VMEM((PEER_HEADS, N_KEYS // PACK16, PACK16, tm), BF16),
                        pltpu.VMEM((PEER_HEADS, N_KEYS, tm), F32),
                        pltpu.VMEM((PEER_HEADS, N_KEYS, tm), F32),
                        pltpu.VMEM((d, tm), F32)],
        compiler_params=_cparams(("arbitrary", "arbitrary")),
        name="peer",
    )(h2t, wq_t, k1, k2, down, up_t, x1, g2, norm_f)


def _rope_tables(rows, cols):
    r, c = jnp.meshgrid(jnp.arange(rows, dtype=F32), jnp.arange(cols, dtype=F32), indexing="ij")
    n_freq = RET_DK // 4
    inv = ROPE_BASE ** (-jnp.arange(n_freq, dtype=F32) / n_freq)
    ang = jnp.concatenate([r.reshape(-1, 1) * inv, c.reshape(-1, 1) * inv], axis=-1)
    cos, sin = jnp.cos(ang), jnp.sin(ang)
    return jnp.concatenate([cos, cos], axis=-1), jnp.concatenate([-sin, sin], axis=-1)


def kernel(x, c, ctx, c_ctx, w_mod, b_mod, norm1_w, norm2_w, w_in, ret_decay_f, ret_decay_b,
           gla_gk_up_f, gla_gk_bias_f, gla_gk_up_b, gla_gk_bias_b, w_out, peer_w_q, peer_k1, peer_k2,
           peer_down, peer_up, norm_f_w):
    depth = w_mod.shape[0]
    assert depth == 1
    b, l, d = x.shape
    lc = ctx.shape[1]
    li = 0

    cc = jnp.zeros((8, d), F32).at[:b].set(c).at[b].set(c_ctx)
    mod = _modulation(cc, w_mod[li], b_mod[li][None, :])
    sh1, sc1, g1, sh2, sc2, g2 = [mod[:b, j * d:(j + 1) * d][:, None, :] for j in range(N_MOD)]
    csh1, csc1 = [jnp.broadcast_to(mod[b:b + 1, j * d:(j + 1) * d][:, None, :], (b, 1, d)) for j in range(2)]

    w_main = w_in[li][:, :MAIN_COLS].astype(BF16)
    w_lr = jnp.zeros((d, LANES), F32).at[:, :2 * GLA_RANK].set(w_in[li][:, MAIN_COLS:])
    gk_up = jnp.zeros((LANES, 2 * GLA_KW), F32)
    gk_up = gk_up.at[:GLA_RANK, :GLA_KW].set(gla_gk_up_f[li])
    gk_up = gk_up.at[GLA_RANK:2 * GLA_RANK, GLA_KW:].set(gla_gk_up_b[li])
    gk_bias = jnp.concatenate([gla_gk_bias_f[li], gla_gk_bias_b[li]])[None, :]
    n1 = norm1_w[li][None, :]
    dec_f = jnp.zeros((8, LANES), F32).at[:RET_HEADS].set(jnp.broadcast_to(ret_decay_f[li][:, None], (RET_HEADS, LANES)))
    dec_b = jnp.zeros((8, LANES), F32).at[:RET_HEADS].set(jnp.broadcast_to(ret_decay_b[li][:, None], (RET_HEADS, LANES)))
    cos, sin = _rope_tables(l // GRID_W, GRID_W)
    cos_c = jnp.ones((lc, RET_DK), F32)
    sin_c = jnp.zeros((lc, RET_DK), F32)

    fc = _inproj(ctx, csh1, csc1, n1, w_main, w_lr, gk_up, gk_bias, cos_c, sin_c, tm=256)
    zr = jnp.zeros((b, RET_HEADS, RET_DK, RET_DK), F32)
    zg = jnp.zeros((b, GLA_HEADS, GLA_DK, GLA_DV), F32)
    ctx_out = _scan((fc[0], fc[1], fc[2], fc[4], fc[5], fc[6], fc[8], fc[9]), dec_f, dec_b,
                    (zr, zr, zg, zg), tb=256)
    ctx_states = ctx_out[4:]

    fl = _inproj(x, sh1, sc1, n1, w_main, w_lr, gk_up, gk_bias, cos, sin, tm=256)
    lat = _scan((fl[0], fl[1], fl[2], fl[4], fl[5], fl[6], fl[8], fl[9]), dec_f, dec_b,
                ctx_states, tb=256)

    x1, h2t = _mix(lat[0], lat[1], lat[2], lat[3], fl[3], fl[7], x, g1, sh2, sc2,
                   norm2_w[li][None, :], w_out[li].astype(BF16), tm=256)

    wq_t = peer_w_q[li].T.astype(BF16)
    return _peer(h2t, wq_t, peer_k1[li].astype(BF16), peer_k2[li].astype(BF16),
                 peer_down[li].astype(BF16), peer_up[li].T.astype(BF16), x1, g2, norm_f_w[None, :],
                 tm=512, ib=4)
```

```python
import functools

import jax
import jax.numpy as jnp
from jax import lax
from jax.experimental import pallas as pl
from jax.experimental.pallas import tpu as pltpu

F32 = jnp.float32
BF16 = jnp.bfloat16
HIGHEST = lax.Precision.HIGHEST

D_MODEL = 1024
GRID_W = 64
RET_HEADS = 4
RET_WIDTH = 512
RET_DK = 128
GLA_HEADS = 4
GLA_WIDTH = 512
GLA_DV = 128
GLA_DK = 64
GLA_KW = GLA_HEADS * GLA_DK
GLA_RANK = 16
GLA_GATE_NORM = 16.0
CHUNK = 64
ROPE_BASE = 10000.0
N_KEYS = 128
PEER_HEADS = 8
PEER_TOPK = 16
PEER_DQ = 256
N_MOD = 6
EPS = 1e-6
MAIN_COLS = 3584
LANES = 128
SUBLANES = 8
PACK16 = 16
MXU_DEPTH = 256
VMEM_LIMIT = 56 * 1024 * 1024


def _cparams(sem):
    return pltpu.CompilerParams(dimension_semantics=sem, vmem_limit_bytes=VMEM_LIMIT)


def _silu(v):
    return v / (1.0 + jnp.exp(-v))


def _log_sigmoid(z):
    return jnp.minimum(z, 0.0) - jnp.log(1.0 + jnp.exp(-jnp.abs(z)))


def _dot(a, b):
    return jnp.dot(a, b, preferred_element_type=F32)


def _dot_nt(a, b):
    return lax.dot_general(a, b, (((1,), (1,)), ((), ())), preferred_element_type=F32)


def _mod_kernel(c_ref, w_ref, b_ref, o_ref):
    a = _silu(c_ref[...])
    o_ref[...] = jnp.dot(a, w_ref[...], precision=HIGHEST, preferred_element_type=F32) + b_ref[...]


def _modulation(cc, w_mod, b_mod):
    n = w_mod.shape[1]
    tn = 1536
    return pl.pallas_call(
        _mod_kernel,
        grid=(n // tn,),
        in_specs=[pl.BlockSpec((8, D_MODEL), lambda j: (0, 0)),
                  pl.BlockSpec((D_MODEL, tn), lambda j: (0, j)),
                  pl.BlockSpec((1, tn), lambda j: (0, j))],
        out_specs=pl.BlockSpec((8, tn), lambda j: (0, j)),
        out_shape=jax.ShapeDtypeStruct((8, n), F32),
        compiler_params=_cparams(("arbitrary",)),
        name="modulation",
    )(cc, w_mod, b_mod)


def _inproj_kernel(x_ref, sh_ref, sc_ref, nw_ref, wm_ref, wl_ref, up_ref, gb_ref, cos_ref, sin_ref,
                   rq_ref, rk_ref, rv_ref, rg_ref, gq_ref, gk_ref, gv_ref, gg_ref, laf_ref, lab_ref):
    x = x_ref[0]
    ms = jnp.mean(x * x, axis=-1, keepdims=True)
    h = x * lax.rsqrt(ms + EPS) * nw_ref[...]
    h = h * (1.0 + sc_ref[0]) + sh_ref[0]
    p = _dot(h.astype(BF16), wm_ref[...])
    cos = cos_ref[...]
    sin = sin_ref[...]
    k_scale = RET_DK ** -0.5
    for hh in range(RET_HEADS):
        lo = hh * RET_DK
        q = p[:, lo:lo + RET_DK]
        k = p[:, RET_WIDTH + lo:RET_WIDTH + lo + RET_DK] * k_scale
        rq_ref[0, :, lo:lo + RET_DK] = q * cos + pltpu.roll(q, RET_DK // 2, axis=1) * sin
        rk_ref[0, :, lo:lo + RET_DK] = k * cos + pltpu.roll(k, RET_DK // 2, axis=1) * sin
    rv_ref[0] = p[:, 1024:1536]
    rg_ref[0] = p[:, 1536:2048]
    gq_ref[0] = p[:, 2048:2304] * (GLA_DK ** -0.5)
    gk_ref[0] = p[:, 2304:2560]
    gv_ref[0] = p[:, 2560:3072]
    gg_ref[0] = p[:, 3072:3584]
    lr = jnp.dot(h, wl_ref[...], precision=HIGHEST, preferred_element_type=F32)
    pre = jnp.dot(lr, up_ref[...], precision=HIGHEST, preferred_element_type=F32) + gb_ref[...]
    la = _log_sigmoid(pre) * (1.0 / GLA_GATE_NORM)
    laf_ref[0] = la[:, :GLA_KW]
    lab_ref[0] = la[:, GLA_KW:]


def _inproj(x, shift, scale, norm_w, w_main, w_lr, gk_up, gk_bias, cos, sin, tm):
    b, l, d = x.shape
    nt = l // tm
    tok = lambda w: pl.BlockSpec((1, tm, w), lambda bi, i: (bi, i, 0))
    full = lambda shp: pl.BlockSpec(shp, lambda bi, i: tuple(0 for _ in shp))
    vec = pl.BlockSpec((1, 1, d), lambda bi, i: (bi, 0, 0))
    widths = (RET_WIDTH, RET_WIDTH, RET_WIDTH, RET_WIDTH, GLA_KW, GLA_KW, GLA_WIDTH, GLA_WIDTH, GLA_KW, GLA_KW)
    return pl.pallas_call(
        _inproj_kernel,
        grid=(b, nt),
        in_specs=[tok(d), vec, vec, full((1, d)), full((d, MAIN_COLS)), full((d, LANES)),
                  full((LANES, 2 * GLA_KW)), full((1, 2 * GLA_KW)),
                  pl.BlockSpec((tm, RET_DK), lambda bi, i: (i, 0)),
                  pl.BlockSpec((tm, RET_DK), lambda bi, i: (i, 0))],
        out_specs=[tok(w) for w in widths],
        out_shape=[jax.ShapeDtypeStruct((b, l, w), F32) for w in widths],
        compiler_params=_cparams(("arbitrary", "arbitrary")),
        name="inproj",
    )(x, shift, scale, norm_w, w_main, w_lr, gk_up, gk_bias, cos, sin)


def _scan_kernel(decf_ref, decb_ref,
                 rqf_ref, rkf_ref, rvf_ref, rqb_ref, rkb_ref, rvb_ref,
                 gqf_ref, gkf_ref, gvf_ref, laf_ref, gqb_ref, gkb_ref, gvb_ref, lab_ref,
                 s0rf_ref, s0rb_ref, s0gf_ref, s0gb_ref,
                 orf_ref, orb_ref, ogf_ref, ogb_ref,
                 srf_out, srb_out, sgf_out, sgb_out,
                 srf, srb, sgf, sgb, *, nch):
    i = pl.program_id(1)

    @pl.when(i == 0)
    def _():
        srf[...] = s0rf_ref[0]
        srb[...] = s0rb_ref[0]
        sgf[...] = s0gf_ref[0]
        sgb[...] = s0gb_ref[0]

    C = CHUNK
    row = lax.broadcasted_iota(jnp.int32, (C, C), 0)
    col = lax.broadcasted_iota(jnp.int32, (C, C), 1)
    rel_f = (row - col).astype(F32)
    rel_b = (col - row).astype(F32)
    causal = row >= col
    anti = col >= row
    tri_lo = jnp.where(causal, 1.0, 0.0).astype(F32)
    tri_up = jnp.where(anti, 1.0, 0.0).astype(F32)
    pos_col = lax.broadcasted_iota(jnp.int32, (C, LANES), 0).astype(F32)

    lgf = _log_sigmoid(decf_ref[...])
    lgb = _log_sigmoid(decb_ref[...])

    def ret_consts(lg, h, fwd):
        lrow = lg[h:h + 1, :]
        rel = rel_f if fwd else rel_b
        mask = jnp.where(rel >= 0, jnp.exp(lrow[:, :C] * jnp.maximum(rel, 0.0)), 0.0)
        if fwd:
            qd = jnp.exp(lrow * (pos_col + 1.0))
            kd = jnp.exp(lrow * (C - 1.0 - pos_col))
        else:
            qd = jnp.exp(lrow * (C - pos_col))
            kd = jnp.exp(lrow * pos_col)
        cd = jnp.exp(lrow * float(C))
        return mask, qd, kd, cd

    cf = [ret_consts(lgf, h, True) for h in range(RET_HEADS)]
    cb = [ret_consts(lgb, h, False) for h in range(RET_HEADS)]

    def ret_chunk(q_ref, k_ref, v_ref, o_ref, s_ref, consts, r0):
        for h in range(RET_HEADS):
            mask, qd, kd, cd = consts[h]
            lo = h * RET_DK
            q = q_ref[0, r0:r0 + C, lo:lo + RET_DK]
            k = k_ref[0, r0:r0 + C, lo:lo + RET_DK]
            v = v_ref[0, r0:r0 + C, lo:lo + RET_DK].astype(BF16)
            s = s_ref[h]
            scores = _dot_nt(q.astype(BF16), k.astype(BF16)) * mask
            o = _dot(scores.astype(BF16), v) + _dot((q * qd).astype(BF16), s.astype(BF16))
            o_ref[0, r0:r0 + C, lo:lo + RET_DK] = o
            s_ref[h] = cd * s + _dot((k * kd).T.astype(BF16), v)

    def gla_chunk(q_ref, k_ref, v_ref, a_ref, o_ref, s_ref, fwd, r0):
        a = a_ref[0, r0:r0 + C, :]
        tri = tri_lo if fwd else tri_up
        g = jnp.dot(tri, a, precision=HIGHEST, preferred_element_type=F32)
        g_last = g[C - 1:C, :] if fwd else g[0:1, :]
        eg = jnp.exp(g)
        qt = q_ref[0, r0:r0 + C, :] * eg
        kk = k_ref[0, r0:r0 + C, :]
        kt = kk * jnp.exp(-g)
        kl = kk * jnp.exp(g_last - g)
        tot_col = jnp.exp(jnp.sum(a.T, axis=1, keepdims=True))
        keep = causal if fwd else anti
        for h in range(GLA_HEADS):
            lo = h * GLA_DK
            vo = h * GLA_DV
            v = v_ref[0, r0:r0 + C, vo:vo + GLA_DV].astype(BF16)
            s = s_ref[h]
            qh = qt[:, lo:lo + GLA_DK].astype(BF16)
            scores = jnp.where(keep, _dot_nt(qh, kt[:, lo:lo + GLA_DK].astype(BF16)), 0.0)
            o = _dot(scores.astype(BF16), v) + _dot(qh, s.astype(BF16))
            o_ref[0, r0:r0 + C, vo:vo + GLA_DV] = o
            s_ref[h] = tot_col[lo:lo + GLA_DK, :] * s + _dot(kl[:, lo:lo + GLA_DK].T.astype(BF16), v)

    for c in range(nch):
        rf = c * C
        rb = (nch - 1 - c) * C
        ret_chunk(rqf_ref, rkf_ref, rvf_ref, orf_ref, srf, cf, rf)
        ret_chunk(rqb_ref, rkb_ref, rvb_ref, orb_ref, srb, cb, rb)
        gla_chunk(gqf_ref, gkf_ref, gvf_ref, laf_ref, ogf_ref, sgf, True, rf)
        gla_chunk(gqb_ref, gkb_ref, gvb_ref, lab_ref, ogb_ref, sgb, False, rb)

    @pl.when(i == pl.num_programs(1) - 1)
    def _():
        srf_out[0] = srf[...]
        srb_out[0] = srb[...]
        sgf_out[0] = sgf[...]
        sgb_out[0] = sgb[...]


def _scan(feats, dec_f, dec_b, states, tb):
    rq, rk, rv, gq, gk, gv, la_f, la_b = feats
    b, l, _ = rq.shape
    nb = l // tb
    fwd = lambda w: pl.BlockSpec((1, tb, w), lambda bi, i: (bi, i, 0))
    bwd = lambda w: pl.BlockSpec((1, tb, w), lambda bi, i: (bi, nb - 1 - i, 0))
    dec = pl.BlockSpec((8, LANES), lambda bi, i: (0, 0))
    st_r = pl.BlockSpec((1, RET_HEADS, RET_DK, RET_DK), lambda bi, i: (bi, 0, 0, 0))
    st_g = pl.BlockSpec((1, GLA_HEADS, GLA_DK, GLA_DV), lambda bi, i: (bi, 0, 0, 0))
    o_shape = jax.ShapeDtypeStruct((b, l, RET_WIDTH), F32)
    sr_shape = jax.ShapeDtypeStruct((b, RET_HEADS, RET_DK, RET_DK), F32)
    sg_shape = jax.ShapeDtypeStruct((b, GLA_HEADS, GLA_DK, GLA_DV), F32)
    return pl.pallas_call(
        functools.partial(_scan_kernel, nch=tb // CHUNK),
        grid=(b, nb),
        in_specs=[dec, dec,
                  fwd(RET_WIDTH), fwd(RET_WIDTH), fwd(RET_WIDTH),
                  bwd(RET_WIDTH), bwd(RET_WIDTH), bwd(RET_WIDTH),
                  fwd(GLA_KW), fwd(GLA_KW), fwd(GLA_WIDTH), fwd(GLA_KW),
                  bwd(GLA_KW), bwd(GLA_KW), bwd(GLA_WIDTH), bwd(GLA_KW),
                  st_r, st_r, st_g, st_g],
        out_specs=[fwd(RET_WIDTH), bwd(RET_WIDTH), fwd(GLA_WIDTH), bwd(GLA_WIDTH),
                   st_r, st_r, st_g, st_g],
        out_shape=[o_shape, o_shape, o_shape, o_shape, sr_shape, sr_shape, sg_shape, sg_shape],
        scratch_shapes=[pltpu.VMEM((RET_HEADS, RET_DK, RET_DK), F32),
                        pltpu.VMEM((RET_HEADS, RET_DK, RET_DK), F32),
                        pltpu.VMEM((GLA_HEADS, GLA_DK, GLA_DV), F32),
                        pltpu.VMEM((GLA_HEADS, GLA_DK, GLA_DV), F32)],
        compiler_params=_cparams(("arbitrary", "arbitrary")),
        name="bidir_scan",
    )(dec_f, dec_b, rq, rk, rv, rq, rk, rv, gq, gk, gv, la_f, gq, gk, gv, la_b, *states)


def _mix_kernel(orf_ref, orb_ref, ogf_ref, ogb_ref, rg_ref, gg_ref, x_ref, g1_ref, sh_ref, sc_ref,
                nw_ref, wo_ref, x1_ref, h2t_ref):
    def head_norm(o):
        parts = []
        for h in range(RET_HEADS):
            oh = o[:, h * 128:(h + 1) * 128]
            parts.append(oh * lax.rsqrt(jnp.mean(oh * oh, axis=-1, keepdims=True) + EPS))
        return jnp.concatenate(parts, axis=-1)

    ret = head_norm(orf_ref[0] + orb_ref[0]) * _silu(rg_ref[0])
    gla = head_norm(ogf_ref[0] + ogb_ref[0]) * _silu(gg_ref[0])
    y = jnp.concatenate([ret, gla], axis=-1).astype(BF16)
    x1 = x_ref[0] + g1_ref[0] * _dot(y, wo_ref[...])
    x1_ref[0] = x1
    ms = jnp.mean(x1 * x1, axis=-1, keepdims=True)
    h2 = x1 * lax.rsqrt(ms + EPS) * nw_ref[...]
    h2 = h2 * (1.0 + sc_ref[0]) + sh_ref[0]
    h2t_ref[...] = h2.T.astype(BF16)


def _mix(o_rf, o_rb, o_gf, o_gb, rg, gg, x, g1, sh2, sc2, norm_w, w_out, tm):
    b, l, d = x.shape
    nt = l // tm
    tok = lambda w: pl.BlockSpec((1, tm, w), lambda bi, i: (bi, i, 0))
    vec = pl.BlockSpec((1, 1, d), lambda bi, i: (bi, 0, 0))
    return pl.pallas_call(
        _mix_kernel,
        grid=(b, nt),
        in_specs=[tok(512)] * 6 + [tok(d), vec, vec, vec,
                                   pl.BlockSpec((1, d), lambda bi, i: (0, 0)),
                                   pl.BlockSpec((d, d), lambda bi, i: (0, 0))],
        out_specs=[tok(d), pl.BlockSpec((d, tm), lambda bi, i: (0, bi * nt + i))],
        out_shape=[jax.ShapeDtypeStruct((b, l, d), F32), jax.ShapeDtypeStruct((d, b * l), BF16)],
        compiler_params=_cparams(("arbitrary", "arbitrary")),
        name="mix_outproj",
    )(o_rf, o_rb, o_gf, o_gb, rg, gg, x, g1, sh2, sc2, norm_w, w_out)


def _oddeven_merge_sort_pairs(n):
    pairs = []
    p = 1
    while p < n:
        k = p
        while k >= 1:
            for j in range(k % p, n - k, 2 * k):
                for i in range(min(k, n - j - k)):
                    if (i + j) // (2 * p) == (i + j + k) // (2 * p):
                        pairs.append((i + j, i + j + k))
            k //= 2
        p *= 2
    return pairs


def _bitonic_merge_pairs(n):
    pairs = []
    d = n // 2
    while d >= 1:
        pairs.extend((i, i + d) for i in range(n) if (i & d) == 0)
        d //= 2
    return pairs


_SORT16 = _oddeven_merge_sort_pairs(PEER_TOPK)
_MERGE16 = _bitonic_merge_pairs(PEER_TOPK)


def _compare_exchange(x, pairs):
    for a, b in pairs:
        hi = jnp.maximum(x[a], x[b])
        lo = jnp.minimum(x[a], x[b])
        x[a] = hi
        x[b] = lo


def _sublane_allreduce(v, op):
    for sh in (4, 2, 1):
        v = op(v, pltpu.roll(v, sh, axis=0))
    return v


def _top16_desc(s):
    x = [s[SUBLANES * k:SUBLANES * (k + 1), :] for k in range(PEER_TOPK)]
    _compare_exchange(x, _SORT16)
    for sh in (4, 2, 1):
        c = [jnp.maximum(x[k], pltpu.roll(x[PEER_TOPK - 1 - k], sh, axis=0)) for k in range(PEER_TOPK)]
        _compare_exchange(c, _MERGE16)
        x = c
    return x


def _route_group(s1, s2):
    inf = jnp.inf
    v1 = _top16_desc(s1)
    v2 = _top16_desc(s2)
    sub = lax.broadcasted_iota(jnp.int32, v1[0].shape, 0)

    def pack_rows(v):
        out = v[0]
        for r in range(1, SUBLANES):
            out = jnp.where(sub == r, v[r], out)
        return out

    v1lo, v1hi = pack_rows(v1[:SUBLANES]), pack_rows(v1[SUBLANES:])
    v2lo, v2hi = pack_rows(v2[:SUBLANES]), pack_rows(v2[SUBLANES:])
    cands = [v1[0] + v2lo, v1[0] + v2hi] + [v1[a] + v2lo for a in range(1, SUBLANES)] + [v1hi + v2[0]]
    c = list(cands)
    tau = None
    for _ in range(PEER_TOPK):
        m = c[0]
        for ci in c[1:]:
            m = jnp.maximum(m, ci)
        tau = _sublane_allreduce(m, jnp.maximum)
        c = [jnp.where(ci == tau, -inf, ci) for ci in c]
    m12 = v1[0] + v2[0]
    zs = None
    for cd in cands:
        t = jnp.where(cd >= tau, jnp.exp(cd - m12), 0.0)
        zs = t if zs is None else zs + t
    zinv = 1.0 / _sublane_allreduce(zs, jnp.add)
    thetas = []
    for b in range(PEER_TOPK):
        lo = jnp.where(v1lo + v2[b] >= tau, v1lo, inf)
        hi = jnp.where(v1hi + v2[b] >= tau, v1hi, inf)
        thetas.append(_sublane_allreduce(jnp.minimum(lo, hi), jnp.minimum))
    cnt, e1, rank, e2 = [], [], [], []
    for k in range(N_KEYS // SUBLANES):
        x = s1[SUBLANES * k:SUBLANES * (k + 1), :]
        y = s2[SUBLANES * k:SUBLANES * (k + 1), :]
        ck = jnp.where(x >= thetas[0], 1.0, 0.0)
        rk = jnp.where(v2[0] > y, 1.0, 0.0)
        for b in range(1, PEER_TOPK):
            ck = ck + jnp.where(x >= thetas[b], 1.0, 0.0)
            rk = rk + jnp.where(v2[b] > y, 1.0, 0.0)
        cnt.append(ck)
        rank.append(rk)
        e1.append(jnp.exp(x - v1[0]) * zinv)
        e2.append(jnp.exp(y - v2[0]))
    return cnt, e1, rank, e2


def _peer_kernel(h2t_ref, wqt_ref, k1_ref, k2_ref, down_ref, upt_ref, x1_ref, g2_ref, nf_ref,
                 out_ref,
                 rank2_s, e2_s, cnt_s, e1_s, acc_s, a_s, h0_s, h1_s, *, ib):
    h_s = (h0_s, h1_s)
    e = pl.program_id(1)
    tm = h2t_ref.shape[1]
    half = PEER_DQ // 2
    nk2 = N_KEYS // PACK16

    @pl.when(e == 0)
    def _():
        acc_s[...] = jnp.zeros_like(acc_s)

        def head_body(h, carry):
            ht = h2t_ref[...]
            row0 = pl.multiple_of(h * PEER_DQ, PEER_DQ)
            q = _dot(wqt_ref[pl.ds(row0, PEER_DQ), :], ht).astype(BF16)
            s1 = _dot(k1_ref[h], q[:half, :])
            s2 = _dot(k2_ref[h], q[half:, :])
            for g in range(tm // LANES):
                sl = slice(g * LANES, (g + 1) * LANES)
                cnt, e1, rank, e2 = _route_group(s1[:, sl], s2[:, sl])
                cnt_s[h, g] = jnp.concatenate(cnt, axis=0)
                e1_s[h, g] = jnp.concatenate(e1, axis=0)
                for k2 in range(nk2):
                    rank2_s[h, k2, :, sl] = jnp.concatenate(rank[2 * k2:2 * k2 + 2], axis=0).astype(BF16)
                    e2_s[h, k2, :, sl] = jnp.concatenate(e2[2 * k2:2 * k2 + 2], axis=0).astype(BF16)
            return carry

        lax.fori_loop(0, PEER_HEADS, head_body, 0)

    ipb = MXU_DEPTH // N_KEYS
    nsub = ib // ipb
    d = acc_s.shape[0]
    dpc = d // ipb

    def activations(k, p):
        r0 = k * MXU_DEPTH + p * N_KEYS
        a_s[k % 2, p * N_KEYS:(p + 1) * N_KEYS, :] = _dot(down_ref[r0:r0 + N_KEYS, :], h2t_ref[...])

    def weighted(k, g):
        i0 = e * ib + k * ipb
        sl = slice(g * LANES, (g + 1) * LANES)
        ws = [jnp.zeros((nk2, PACK16, LANES), BF16) for _ in range(ipb)]
        for h in range(PEER_HEADS):
            r = rank2_s[h, :, :, sl]
            v = e2_s[h, :, :, sl]
            for p in range(ipb):
                c_row = jnp.broadcast_to(cnt_s[h, g, pl.ds(i0 + p, 1), :], (PACK16, LANES)).astype(BF16)
                e_row = jnp.broadcast_to(e1_s[h, g, pl.ds(i0 + p, 1), :], (PACK16, LANES)).astype(BF16)
                ws[p] = ws[p] + jnp.where(r < c_row[None], v, jnp.zeros((), BF16)) * e_row[None]
        for p in range(ipb):
            a = a_s[k % 2, p * N_KEYS:(p + 1) * N_KEYS, sl]
            act = 0.5 * a * (1.0 + lax.erf(a * (2.0 ** -0.5)))
            hw = ws[p] * act.astype(BF16).reshape(nk2, PACK16, LANES)
            h_s[k % 2][p * N_KEYS:(p + 1) * N_KEYS, sl] = hw.reshape(N_KEYS, LANES)

    def up_project(k, p):
        c0 = k * MXU_DEPTH
        acc_s[p * dpc:(p + 1) * dpc, :] += _dot(upt_ref[p * dpc:(p + 1) * dpc, c0:c0 + MXU_DEPTH], h_s[k % 2][...])

    ngroups = tm // LANES
    for p in range(ipb):
        activations(0, p)
    for k in range(nsub):
        for g in range(ngroups):
            p = g * ipb // ngroups
            first = (g * ipb) % ngroups == 0
            if first and k + 1 < nsub:
                activations(k + 1, p)
            weighted(k, g)
        for p in range(ipb):
            up_project(k, p)

    @pl.when(e == pl.num_programs(1) - 1)
    def _():
        y = x1_ref[0] + g2_ref[0] * acc_s[...].T
        ms = jnp.mean(y * y, axis=-1, keepdims=True)
        out_ref[0] = y * lax.rsqrt(ms + EPS) * nf_ref[...]


def _peer(h2t, wq_t, k1, k2, down, up_t, x1, g2, norm_f, tm, ib):
    b, l, d = x1.shape
    nt = l // tm
    ne = N_KEYS // ib
    te = ib * N_KEYS
    const = lambda shp: pl.BlockSpec(shp, lambda t, e: tuple(0 for _ in shp))
    return pl.pallas_call(
        functools.partial(_peer_kernel, ib=ib),
        grid=(b * nt, ne),
        in_specs=[pl.BlockSpec((d, tm), lambda t, e: (0, t)),
                  const((PEER_HEADS * PEER_DQ, d)),
                  const((PEER_HEADS, N_KEYS, PEER_DQ // 2)),
                  const((PEER_HEADS, N_KEYS, PEER_DQ // 2)),
                  pl.BlockSpec((te, d), lambda t, e: (e, 0)),
                  pl.BlockSpec((d, te), lambda t, e: (0, e)),
                  pl.BlockSpec((1, tm, d), lambda t, e: (t // nt, t % nt, 0)),
                  pl.BlockSpec((1, 1, d), lambda t, e: (t // nt, 0, 0)),
                  const((1, d))],
        out_specs=pl.BlockSpec((1, tm, d), lambda t, e: (t // nt, t % nt, 0)),
        out_shape=jax.ShapeDtypeStruct((b, l, d), F32),
        scratch_shapes=[pltpu.VMEM((PEER_HEADS, N_KEYS // PACK16, PACK16, tm), BF16),
                        pltpu.VMEM((PEER_HEADS, N_KEYS // PACK16, PACK16, tm), BF16),
                        pltpu.VMEM((PEER_HEADS, tm // LANES, N_KEYS, LANES), F32),
                        pltpu.VMEM((PEER_HEADS, tm // LANES, N_KEYS, LANES), F32),
                        pltpu.VMEM((d, tm), F32),
                        pltpu.VMEM((2, MXU_DEPTH, tm), F32),
                        pltpu.VMEM((MXU_DEPTH, tm), BF16),
                        pltpu.VMEM((MXU_DEPTH, tm), BF16)],
        compiler_params=_cparams(("arbitrary", "arbitrary")),
        name="peer",
    )(h2t, wq_t, k1, k2, down, up_t, x1, g2, norm_f)


def _rope_tables(rows, cols):
    r, c = jnp.meshgrid(jnp.arange(rows, dtype=F32), jnp.arange(cols, dtype=F32), indexing="ij")
    n_freq = RET_DK // 4
    inv = ROPE_BASE ** (-jnp.arange(n_freq, dtype=F32) / n_freq)
    ang = jnp.concatenate([r.reshape(-1, 1) * inv, c.reshape(-1, 1) * inv], axis=-1)
    cos, sin = jnp.cos(ang), jnp.sin(ang)
    return jnp.concatenate([cos, cos], axis=-1), jnp.concatenate([-sin, sin], axis=-1)


def kernel(x, c, ctx, c_ctx, w_mod, b_mod, norm1_w, norm2_w, w_in, ret_decay_f, ret_decay_b,
           gla_gk_up_f, gla_gk_bias_f, gla_gk_up_b, gla_gk_bias_b, w_out, peer_w_q, peer_k1, peer_k2,
           peer_down, peer_up, norm_f_w):
    depth = w_mod.shape[0]
    assert depth == 1
    b, l, d = x.shape
    lc = ctx.shape[1]
    li = 0

    cc = jnp.zeros((8, d), F32).at[:b].set(c).at[b].set(c_ctx)
    mod = _modulation(cc, w_mod[li], b_mod[li][None, :])
    sh1, sc1, g1, sh2, sc2, g2 = [mod[:b, j * d:(j + 1) * d][:, None, :] for j in range(N_MOD)]
    csh1, csc1 = [jnp.broadcast_to(mod[b:b + 1, j * d:(j + 1) * d][:, None, :], (b, 1, d)) for j in range(2)]

    w_main = w_in[li][:, :MAIN_COLS].astype(BF16)
    w_lr = jnp.zeros((d, LANES), F32).at[:, :2 * GLA_RANK].set(w_in[li][:, MAIN_COLS:])
    gk_up = jnp.zeros((LANES, 2 * GLA_KW), F32)
    gk_up = gk_up.at[:GLA_RANK, :GLA_KW].set(gla_gk_up_f[li])
    gk_up = gk_up.at[GLA_RANK:2 * GLA_RANK, GLA_KW:].set(gla_gk_up_b[li])
    gk_bias = jnp.concatenate([gla_gk_bias_f[li], gla_gk_bias_b[li]])[None, :]
    n1 = norm1_w[li][None, :]
    dec_f = jnp.zeros((8, LANES), F32).at[:RET_HEADS].set(jnp.broadcast_to(ret_decay_f[li][:, None], (RET_HEADS, LANES)))
    dec_b = jnp.zeros((8, LANES), F32).at[:RET_HEADS].set(jnp.broadcast_to(ret_decay_b[li][:, None], (RET_HEADS, LANES)))
    cos, sin = _rope_tables(l // GRID_W, GRID_W)
    cos_c = jnp.ones((lc, RET_DK), F32)
    sin_c = jnp.zeros((lc, RET_DK), F32)

    fc = _inproj(ctx, csh1, csc1, n1, w_main, w_lr, gk_up, gk_bias, cos_c, sin_c, tm=256)
    zr = jnp.zeros((b, RET_HEADS, RET_DK, RET_DK), F32)
    zg = jnp.zeros((b, GLA_HEADS, GLA_DK, GLA_DV), F32)
    ctx_out = _scan((fc[0], fc[1], fc[2], fc[4], fc[5], fc[6], fc[8], fc[9]), dec_f, dec_b,
                    (zr, zr, zg, zg), tb=256)
    ctx_states = ctx_out[4:]

    fl = _inproj(x, sh1, sc1, n1, w_main, w_lr, gk_up, gk_bias, cos, sin, tm=256)
    lat = _scan((fl[0], fl[1], fl[2], fl[4], fl[5], fl[6], fl[8], fl[9]), dec_f, dec_b,
                ctx_states, tb=256)

    x1, h2t = _mix(lat[0], lat[1], lat[2], lat[3], fl[3], fl[7], x, g1, sh2, sc2,
                   norm2_w[li][None, :], w_out[li].astype(BF16), tm=256)

    wq_t = peer_w_q[li].T.astype(BF16)
    return _peer(h2t, wq_t, peer_k1[li].astype(BF16), peer_k2[li].astype(BF16),
                 peer_down[li].astype(BF16), peer_up[li].T.astype(BF16), x1, g2, norm_f_w[None, :],
                 tm=512, ib=16)
```

```python
import functools

import jax
import jax.numpy as jnp
from jax import lax
from jax.experimental import pallas as pl
from jax.experimental.pallas import tpu as pltpu

F32 = jnp.float32
BF16 = jnp.bfloat16
HIGHEST = lax.Precision.HIGHEST

D_MODEL = 1024
GRID_W = 64
RET_HEADS = 4
RET_WIDTH = 512
RET_DK = 128
GLA_HEADS = 4
GLA_WIDTH = 512
GLA_DV = 128
GLA_DK = 64
GLA_KW = GLA_HEADS * GLA_DK
GLA_RANK = 16
GLA_GATE_NORM = 16.0
CHUNK = 64
ROPE_BASE = 10000.0
N_KEYS = 128
PEER_HEADS = 8
PEER_TOPK = 16
PEER_DQ = 256
N_MOD = 6
EPS = 1e-6
MAIN_COLS = 3584
LANES = 128
SUBLANES = 8
PACK16 = 16
MXU_DEPTH = 256
VMEM_LIMIT = 56 * 1024 * 1024


def _cparams(sem):
    return pltpu.CompilerParams(dimension_semantics=sem, vmem_limit_bytes=VMEM_LIMIT)


def _silu(v):
    return v / (1.0 + jnp.exp(-v))


def _log_sigmoid(z):
    return jnp.minimum(z, 0.0) - jnp.log(1.0 + jnp.exp(-jnp.abs(z)))


def _dot(a, b):
    return jnp.dot(a, b, preferred_element_type=F32)


def _dot_nt(a, b):
    return lax.dot_general(a, b, (((1,), (1,)), ((), ())), preferred_element_type=F32)


def _mod_kernel(c_ref, w_ref, b_ref, o_ref):
    a = _silu(c_ref[...])
    o_ref[...] = jnp.dot(a, w_ref[...], precision=HIGHEST, preferred_element_type=F32) + b_ref[...]


def _modulation(cc, w_mod, b_mod):
    n = w_mod.shape[1]
    tn = 1536
    return pl.pallas_call(
        _mod_kernel,
        grid=(n // tn,),
        in_specs=[pl.BlockSpec((8, D_MODEL), lambda j: (0, 0)),
                  pl.BlockSpec((D_MODEL, tn), lambda j: (0, j)),
                  pl.BlockSpec((1, tn), lambda j: (0, j))],
        out_specs=pl.BlockSpec((8, tn), lambda j: (0, j)),
        out_shape=jax.ShapeDtypeStruct((8, n), F32),
        compiler_params=_cparams(("arbitrary",)),
        name="modulation",
    )(cc, w_mod, b_mod)


def _inproj_kernel(x_ref, sh_ref, sc_ref, nw_ref, wm_ref, wl_ref, up_ref, gb_ref, cos_ref, sin_ref,
                   rq_ref, rk_ref, rv_ref, rg_ref, gq_ref, gk_ref, gv_ref, gg_ref, laf_ref, lab_ref):
    x = x_ref[0]
    ms = jnp.mean(x * x, axis=-1, keepdims=True)
    h = x * lax.rsqrt(ms + EPS) * nw_ref[...]
    h = h * (1.0 + sc_ref[0]) + sh_ref[0]
    p = _dot(h.astype(BF16), wm_ref[...])
    cos = cos_ref[...]
    sin = sin_ref[...]
    k_scale = RET_DK ** -0.5
    for hh in range(RET_HEADS):
        lo = hh * RET_DK
        q = p[:, lo:lo + RET_DK]
        k = p[:, RET_WIDTH + lo:RET_WIDTH + lo + RET_DK] * k_scale
        rq_ref[0, :, lo:lo + RET_DK] = q * cos + pltpu.roll(q, RET_DK // 2, axis=1) * sin
        rk_ref[0, :, lo:lo + RET_DK] = k * cos + pltpu.roll(k, RET_DK // 2, axis=1) * sin
    rv_ref[0] = p[:, 1024:1536]
    rg_ref[0] = p[:, 1536:2048]
    gq_ref[0] = p[:, 2048:2304] * (GLA_DK ** -0.5)
    gk_ref[0] = p[:, 2304:2560]
    gv_ref[0] = p[:, 2560:3072]
    gg_ref[0] = p[:, 3072:3584]
    lr = jnp.dot(h, wl_ref[...], precision=HIGHEST, preferred_element_type=F32)
    pre = jnp.dot(lr, up_ref[...], precision=HIGHEST, preferred_element_type=F32) + gb_ref[...]
    la = _log_sigmoid(pre) * (1.0 / GLA_GATE_NORM)
    laf_ref[0] = la[:, :GLA_KW]
    lab_ref[0] = la[:, GLA_KW:]


def _inproj(x, shift, scale, norm_w, w_main, w_lr, gk_up, gk_bias, cos, sin, tm):
    b, l, d = x.shape
    nt = l // tm
    tok = lambda w: pl.BlockSpec((1, tm, w), lambda bi, i: (bi, i, 0))
    full = lambda shp: pl.BlockSpec(shp, lambda bi, i: tuple(0 for _ in shp))
    vec = pl.BlockSpec((1, 1, d), lambda bi, i: (bi, 0, 0))
    widths = (RET_WIDTH, RET_WIDTH, RET_WIDTH, RET_WIDTH, GLA_KW, GLA_KW, GLA_WIDTH, GLA_WIDTH, GLA_KW, GLA_KW)
    return pl.pallas_call(
        _inproj_kernel,
        grid=(b, nt),
        in_specs=[tok(d), vec, vec, full((1, d)), full((d, MAIN_COLS)), full((d, LANES)),
                  full((LANES, 2 * GLA_KW)), full((1, 2 * GLA_KW)),
                  pl.BlockSpec((tm, RET_DK), lambda bi, i: (i, 0)),
                  pl.BlockSpec((tm, RET_DK), lambda bi, i: (i, 0))],
        out_specs=[tok(w) for w in widths],
        out_shape=[jax.ShapeDtypeStruct((b, l, w), F32) for w in widths],
        compiler_params=_cparams(("arbitrary", "arbitrary")),
        name="inproj",
    )(x, shift, scale, norm_w, w_main, w_lr, gk_up, gk_bias, cos, sin)


def _scan_kernel(decf_ref, decb_ref,
                 rqf_ref, rkf_ref, rvf_ref, rqb_ref, rkb_ref, rvb_ref,
                 gqf_ref, gkf_ref, gvf_ref, laf_ref, gqb_ref, gkb_ref, gvb_ref, lab_ref,
                 s0rf_ref, s0rb_ref, s0gf_ref, s0gb_ref,
                 orf_ref, orb_ref, ogf_ref, ogb_ref,
                 srf_out, srb_out, sgf_out, sgb_out,
                 srf, srb, sgf, sgb, *, nch):
    i = pl.program_id(1)

    @pl.when(i == 0)
    def _():
        srf[...] = s0rf_ref[0]
        srb[...] = s0rb_ref[0]
        sgf[...] = s0gf_ref[0]
        sgb[...] = s0gb_ref[0]

    C = CHUNK
    row = lax.broadcasted_iota(jnp.int32, (C, C), 0)
    col = lax.broadcasted_iota(jnp.int32, (C, C), 1)
    rel_f = (row - col).astype(F32)
    rel_b = (col - row).astype(F32)
    causal = row >= col
    anti = col >= row
    tri_lo = jnp.where(causal, 1.0, 0.0).astype(F32)
    tri_up = jnp.where(anti, 1.0, 0.0).astype(F32)
    pos_col = lax.broadcasted_iota(jnp.int32, (C, LANES), 0).astype(F32)

    lgf = _log_sigmoid(decf_ref[...])
    lgb = _log_sigmoid(decb_ref[...])

    def ret_consts(lg, h, fwd):
        lrow = lg[h:h + 1, :]
        rel = rel_f if fwd else rel_b
        mask = jnp.where(rel >= 0, jnp.exp(lrow[:, :C] * jnp.maximum(rel, 0.0)), 0.0)
        if fwd:
            qd = jnp.exp(lrow * (pos_col + 1.0))
            kd = jnp.exp(lrow * (C - 1.0 - pos_col))
        else:
            qd = jnp.exp(lrow * (C - pos_col))
            kd = jnp.exp(lrow * pos_col)
        cd = jnp.exp(lrow * float(C))
        return mask, qd, kd, cd

    cf = [ret_consts(lgf, h, True) for h in range(RET_HEADS)]
    cb = [ret_consts(lgb, h, False) for h in range(RET_HEADS)]

    def ret_chunk(q_ref, k_ref, v_ref, o_ref, s_ref, consts, r0):
        for h in range(RET_HEADS):
            mask, qd, kd, cd = consts[h]
            lo = h * RET_DK
            q = q_ref[0, r0:r0 + C, lo:lo + RET_DK]
            k = k_ref[0, r0:r0 + C, lo:lo + RET_DK]
            v = v_ref[0, r0:r0 + C, lo:lo + RET_DK].astype(BF16)
            s = s_ref[h]
            scores = _dot_nt(q.astype(BF16), k.astype(BF16)) * mask
            o = _dot(scores.astype(BF16), v) + _dot((q * qd).astype(BF16), s.astype(BF16))
            o_ref[0, r0:r0 + C, lo:lo + RET_DK] = o
            s_ref[h] = cd * s + _dot((k * kd).T.astype(BF16), v)

    def gla_chunk(q_ref, k_ref, v_ref, a_ref, o_ref, s_ref, fwd, r0):
        a = a_ref[0, r0:r0 + C, :]
        tri = tri_lo if fwd else tri_up
        g = jnp.dot(tri, a, precision=HIGHEST, preferred_element_type=F32)
        g_last = g[C - 1:C, :] if fwd else g[0:1, :]
        eg = jnp.exp(g)
        qt = q_ref[0, r0:r0 + C, :] * eg
        kk = k_ref[0, r0:r0 + C, :]
        kt = kk * jnp.exp(-g)
        kl = kk * jnp.exp(g_last - g)
        tot_col = jnp.exp(jnp.sum(a.T, axis=1, keepdims=True))
        keep = causal if fwd else anti
        for h in range(GLA_HEADS):
            lo = h * GLA_DK
            vo = h * GLA_DV
            v = v_ref[0, r0:r0 + C, vo:vo + GLA_DV].astype(BF16)
            s = s_ref[h]
            qh = qt[:, lo:lo + GLA_DK].astype(BF16)
            scores = jnp.where(keep, _dot_nt(qh, kt[:, lo:lo + GLA_DK].astype(BF16)), 0.0)
            o = _dot(scores.astype(BF16), v) + _dot(qh, s.astype(BF16))
            o_ref[0, r0:r0 + C, vo:vo + GLA_DV] = o
            s_ref[h] = tot_col[lo:lo + GLA_DK, :] * s + _dot(kl[:, lo:lo + GLA_DK].T.astype(BF16), v)

    for c in range(nch):
        rf = c * C
        rb = (nch - 1 - c) * C
        ret_chunk(rqf_ref, rkf_ref, rvf_ref, orf_ref, srf, cf, rf)
        ret_chunk(rqb_ref, rkb_ref, rvb_ref, orb_ref, srb, cb, rb)
        gla_chunk(gqf_ref, gkf_ref, gvf_ref, laf_ref, ogf_ref, sgf, True, rf)
        gla_chunk(gqb_ref, gkb_ref, gvb_ref, lab_ref, ogb_ref, sgb, False, rb)

    @pl.when(i == pl.num_programs(1) - 1)
    def _():
        srf_out[0] = srf[...]
        srb_out[0] = srb[...]
        sgf_out[0] = sgf[...]
        sgb_out[0] = sgb[...]


def _scan(feats, dec_f, dec_b, states, tb):
    rq, rk, rv, gq, gk, gv, la_f, la_b = feats
    b, l, _ = rq.shape
    nb = l // tb
    fwd = lambda w: pl.BlockSpec((1, tb, w), lambda bi, i: (bi, i, 0))
    bwd = lambda w: pl.BlockSpec((1, tb, w), lambda bi, i: (bi, nb - 1 - i, 0))
    dec = pl.BlockSpec((8, LANES), lambda bi, i: (0, 0))
    st_r = pl.BlockSpec((1, RET_HEADS, RET_DK, RET_DK), lambda bi, i: (bi, 0, 0, 0))
    st_g = pl.BlockSpec((1, GLA_HEADS, GLA_DK, GLA_DV), lambda bi, i: (bi, 0, 0, 0))
    o_shape = jax.ShapeDtypeStruct((b, l, RET_WIDTH), F32)
    sr_shape = jax.ShapeDtypeStruct((b, RET_HEADS, RET_DK, RET_DK), F32)
    sg_shape = jax.ShapeDtypeStruct((b, GLA_HEADS, GLA_DK, GLA_DV), F32)
    return pl.pallas_call(
        functools.partial(_scan_kernel, nch=tb // CHUNK),
        grid=(b, nb),
        in_specs=[dec, dec,
                  fwd(RET_WIDTH), fwd(RET_WIDTH), fwd(RET_WIDTH),
                  bwd(RET_WIDTH), bwd(RET_WIDTH), bwd(RET_WIDTH),
                  fwd(GLA_KW), fwd(GLA_KW), fwd(GLA_WIDTH), fwd(GLA_KW),
                  bwd(GLA_KW), bwd(GLA_KW), bwd(GLA_WIDTH), bwd(GLA_KW),
                  st_r, st_r, st_g, st_g],
        out_specs=[fwd(RET_WIDTH), bwd(RET_WIDTH), fwd(GLA_WIDTH), bwd(GLA_WIDTH),
                   st_r, st_r, st_g, st_g],
        out_shape=[o_shape, o_shape, o_shape, o_shape, sr_shape, sr_shape, sg_shape, sg_shape],
        scratch_shapes=[pltpu.VMEM((RET_HEADS, RET_DK, RET_DK), F32),
                        pltpu.VMEM((RET_HEADS, RET_DK, RET_DK), F32),
                        pltpu.VMEM((GLA_HEADS, GLA_DK, GLA_DV), F32),
                        pltpu.VMEM((GLA_HEADS, GLA_DK, GLA_DV), F32)],
        compiler_params=_cparams(("arbitrary", "arbitrary")),
        name="bidir_scan",
    )(dec_f, dec_b, rq, rk, rv, rq, rk, rv, gq, gk, gv, la_f, gq, gk, gv, la_b, *states)


def _mix_kernel(orf_ref, orb_ref, ogf_ref, ogb_ref, rg_ref, gg_ref, x_ref, g1_ref, sh_ref, sc_ref,
                nw_ref, wo_ref, x1_ref, h2t_ref):
    def head_norm(o):
        parts = []
        for h in range(RET_HEADS):
            oh = o[:, h * 128:(h + 1) * 128]
            parts.append(oh * lax.rsqrt(jnp.mean(oh * oh, axis=-1, keepdims=True) + EPS))
        return jnp.concatenate(parts, axis=-1)

    ret = head_norm(orf_ref[0] + orb_ref[0]) * _silu(rg_ref[0])
    gla = head_norm(ogf_ref[0] + ogb_ref[0]) * _silu(gg_ref[0])
    y = jnp.concatenate([ret, gla], axis=-1).astype(BF16)
    x1 = x_ref[0] + g1_ref[0] * _dot(y, wo_ref[...])
    x1_ref[0] = x1
    ms = jnp.mean(x1 * x1, axis=-1, keepdims=True)
    h2 = x1 * lax.rsqrt(ms + EPS) * nw_ref[...]
    h2 = h2 * (1.0 + sc_ref[0]) + sh_ref[0]
    h2t_ref[...] = h2.T.astype(BF16)


def _mix(o_rf, o_rb, o_gf, o_gb, rg, gg, x, g1, sh2, sc2, norm_w, w_out, tm):
    b, l, d = x.shape
    nt = l // tm
    tok = lambda w: pl.BlockSpec((1, tm, w), lambda bi, i: (bi, i, 0))
    vec = pl.BlockSpec((1, 1, d), lambda bi, i: (bi, 0, 0))
    return pl.pallas_call(
        _mix_kernel,
        grid=(b, nt),
        in_specs=[tok(512)] * 6 + [tok(d), vec, vec, vec,
                                   pl.BlockSpec((1, d), lambda bi, i: (0, 0)),
                                   pl.BlockSpec((d, d), lambda bi, i: (0, 0))],
        out_specs=[tok(d), pl.BlockSpec((d, tm), lambda bi, i: (0, bi * nt + i))],
        out_shape=[jax.ShapeDtypeStruct((b, l, d), F32), jax.ShapeDtypeStruct((d, b * l), BF16)],
        compiler_params=_cparams(("arbitrary", "arbitrary")),
        name="mix_outproj",
    )(o_rf, o_rb, o_gf, o_gb, rg, gg, x, g1, sh2, sc2, norm_w, w_out)


def _oddeven_merge_sort_pairs(n):
    pairs = []
    p = 1
    while p < n:
        k = p
        while k >= 1:
            for j in range(k % p, n - k, 2 * k):
                for i in range(min(k, n - j - k)):
                    if (i + j) // (2 * p) == (i + j + k) // (2 * p):
                        pairs.append((i + j, i + j + k))
            k //= 2
        p *= 2
    return pairs


def _bitonic_merge_pairs(n):
    pairs = []
    d = n // 2
    while d >= 1:
        pairs.extend((i, i + d) for i in range(n) if (i & d) == 0)
        d //= 2
    return pairs


_SORT16 = _oddeven_merge_sort_pairs(PEER_TOPK)
_MERGE16 = _bitonic_merge_pairs(PEER_TOPK)


def _compare_exchange(x, pairs):
    for a, b in pairs:
        hi = jnp.maximum(x[a], x[b])
        lo = jnp.minimum(x[a], x[b])
        x[a] = hi
        x[b] = lo


def _sublane_allreduce(v, op):
    for sh in (4, 2, 1):
        v = op(v, pltpu.roll(v, sh, axis=0))
    return v


def _top16_desc(s):
    x = [s[SUBLANES * k:SUBLANES * (k + 1), :] for k in range(PEER_TOPK)]
    _compare_exchange(x, _SORT16)
    for sh in (4, 2, 1):
        c = [jnp.maximum(x[k], pltpu.roll(x[PEER_TOPK - 1 - k], sh, axis=0)) for k in range(PEER_TOPK)]
        _compare_exchange(c, _MERGE16)
        x = c
    return x


def _route_group(s1, s2):
    inf = jnp.inf
    v1 = _top16_desc(s1)
    v2 = _top16_desc(s2)
    sub = lax.broadcasted_iota(jnp.int32, v1[0].shape, 0)

    def pack_rows(v):
        out = v[0]
        for r in range(1, SUBLANES):
            out = jnp.where(sub == r, v[r], out)
        return out

    v1lo, v1hi = pack_rows(v1[:SUBLANES]), pack_rows(v1[SUBLANES:])
    v2lo, v2hi = pack_rows(v2[:SUBLANES]), pack_rows(v2[SUBLANES:])
    cands = [v1[0] + v2lo, v1[0] + v2hi] + [v1[a] + v2lo for a in range(1, SUBLANES)] + [v1hi + v2[0]]
    c = list(cands)
    tau = None
    for _ in range(PEER_TOPK):
        m = c[0]
        for ci in c[1:]:
            m = jnp.maximum(m, ci)
        tau = _sublane_allreduce(m, jnp.maximum)
        c = [jnp.where(ci == tau, -inf, ci) for ci in c]
    m12 = v1[0] + v2[0]
    zs = None
    for cd in cands:
        t = jnp.where(cd >= tau, jnp.exp(cd - m12), 0.0)
        zs = t if zs is None else zs + t
    zinv = 1.0 / _sublane_allreduce(zs, jnp.add)
    thetas = []
    for b in range(PEER_TOPK):
        lo = jnp.where(v1lo + v2[b] >= tau, v1lo, inf)
        hi = jnp.where(v1hi + v2[b] >= tau, v1hi, inf)
        thetas.append(_sublane_allreduce(jnp.minimum(lo, hi), jnp.minimum))
    cnt, e1, rank, e2 = [], [], [], []
    for k in range(N_KEYS // SUBLANES):
        x = s1[SUBLANES * k:SUBLANES * (k + 1), :]
        y = s2[SUBLANES * k:SUBLANES * (k + 1), :]
        ck = jnp.where(x >= thetas[0], 1.0, 0.0)
        rk = jnp.where(v2[0] > y, 1.0, 0.0)
        for b in range(1, PEER_TOPK):
            ck = ck + jnp.where(x >= thetas[b], 1.0, 0.0)
            rk = rk + jnp.where(v2[b] > y, 1.0, 0.0)
        cnt.append(ck)
        rank.append(rk)
        e1.append(jnp.exp(x - v1[0]) * zinv)
        e2.append(jnp.exp(y - v2[0]))
    return cnt, e1, rank, e2


def _peer_kernel(h2t_ref, wqt_ref, k1_ref, k2_ref, down_ref, upt_ref, x1_ref, g2_ref, nf_ref,
                 out_ref,
                 rank2_s, e2_s, cnt_s, e1_s, acc_s, a_s, h0_s, h1_s, *, ib):
    h_s = (h0_s, h1_s)
    e = pl.program_id(1)
    tm = h2t_ref.shape[1]
    half = PEER_DQ // 2
    nk2 = N_KEYS // PACK16

    @pl.when(e == 0)
    def _():
        acc_s[...] = jnp.zeros_like(acc_s)

        def head_body(h, carry):
            ht = h2t_ref[...]
            row0 = pl.multiple_of(h * PEER_DQ, PEER_DQ)
            q = _dot(wqt_ref[pl.ds(row0, PEER_DQ), :], ht).astype(BF16)
            s1 = _dot(k1_ref[h], q[:half, :])
            s2 = _dot(k2_ref[h], q[half:, :])
            for g in range(tm // LANES):
                sl = slice(g * LANES, (g + 1) * LANES)
                cnt, e1, rank, e2 = _route_group(s1[:, sl], s2[:, sl])
                cnt_s[h, g] = jnp.concatenate(cnt, axis=0)
                e1_s[h, g] = jnp.concatenate(e1, axis=0)
                for k2 in range(nk2):
                    rank2_s[h, g, k2] = jnp.concatenate(rank[2 * k2:2 * k2 + 2], axis=0).astype(BF16)
                    e2_s[h, g, k2] = jnp.concatenate(e2[2 * k2:2 * k2 + 2], axis=0).astype(BF16)
            return carry

        lax.fori_loop(0, PEER_HEADS, head_body, 0)

    ipb = MXU_DEPTH // N_KEYS
    nsub = ib // ipb
    d = acc_s.shape[0]
    dpc = d // ipb
    ngroups = tm // LANES

    def activations(k, p):
        r0 = k * MXU_DEPTH + p * N_KEYS
        res = _dot(down_ref[r0:r0 + N_KEYS, :], h2t_ref[...])
        for g in range(ngroups):
            a_s[k % 2, g, p * N_KEYS:(p + 1) * N_KEYS, :] = res[:, g * LANES:(g + 1) * LANES]

    def weighted(k, g):
        i0 = e * ib + k * ipb
        ws = [jnp.zeros((nk2, PACK16, LANES), BF16) for _ in range(ipb)]
        for h in range(PEER_HEADS):
            r = rank2_s[h, g]
            v = e2_s[h, g]
            for p in range(ipb):
                c_row = jnp.broadcast_to(cnt_s[h, g, pl.ds(i0 + p, 1), :], (PACK16, LANES)).astype(BF16)
                e_row = jnp.broadcast_to(e1_s[h, g, pl.ds(i0 + p, 1), :], (PACK16, LANES)).astype(BF16)
                ws[p] = ws[p] + jnp.where(r < c_row[None], v, jnp.zeros((), BF16)) * e_row[None]
        for p in range(ipb):
            a = a_s[k % 2, g, p * N_KEYS:(p + 1) * N_KEYS, :]
            act = 0.5 * a * (1.0 + lax.erf(a * (2.0 ** -0.5)))
            hw = ws[p] * act.astype(BF16).reshape(nk2, PACK16, LANES)
            h_s[k % 2][g, p * N_KEYS:(p + 1) * N_KEYS, :] = hw.reshape(N_KEYS, LANES)

    def up_project(k, p):
        c0 = k * MXU_DEPTH
        rhs = jnp.concatenate([h_s[k % 2][g] for g in range(ngroups)], axis=1)
        acc_s[p * dpc:(p + 1) * dpc, :] += _dot(upt_ref[p * dpc:(p + 1) * dpc, c0:c0 + MXU_DEPTH], rhs)

    for p in range(ipb):
        activations(0, p)
    for k in range(nsub):
        for g in range(ngroups):
            p = g * ipb // ngroups
            first = (g * ipb) % ngroups == 0
            if first and k + 1 < nsub:
                activations(k + 1, p)
            weighted(k, g)
        for p in range(ipb):
            up_project(k, p)

    @pl.when(e == pl.num_programs(1) - 1)
    def _():
        y = x1_ref[0] + g2_ref[0] * acc_s[...].T
        ms = jnp.mean(y * y, axis=-1, keepdims=True)
        out_ref[0] = y * lax.rsqrt(ms + EPS) * nf_ref[...]


def _peer(h2t, wq_t, k1, k2, down, up_t, x1, g2, norm_f, tm, ib):
    b, l, d = x1.shape
    nt = l // tm
    ne = N_KEYS // ib
    te = ib * N_KEYS
    const = lambda shp: pl.BlockSpec(shp, lambda t, e: tuple(0 for _ in shp))
    return pl.pallas_call(
        functools.partial(_peer_kernel, ib=ib),
        grid=(b * nt, ne),
        in_specs=[pl.BlockSpec((d, tm), lambda t, e: (0, t)),
                  const((PEER_HEADS * PEER_DQ, d)),
                  const((PEER_HEADS, N_KEYS, PEER_DQ // 2)),
                  const((PEER_HEADS, N_KEYS, PEER_DQ // 2)),
                  pl.BlockSpec((te, d), lambda t, e: (e, 0)),
                  pl.BlockSpec((d, te), lambda t, e: (0, e)),
                  pl.BlockSpec((1, tm, d), lambda t, e: (t // nt, t % nt, 0)),
                  pl.BlockSpec((1, 1, d), lambda t, e: (t // nt, 0, 0)),
                  const((1, d))],
        out_specs=pl.BlockSpec((1, tm, d), lambda t, e: (t // nt, t % nt, 0)),
        out_shape=jax.ShapeDtypeStruct((b, l, d), F32),
        scratch_shapes=[pltpu.VMEM((PEER_HEADS, tm // LANES, N_KEYS // PACK16, PACK16, LANES), BF16),
                        pltpu.VMEM((PEER_HEADS, tm // LANES, N_KEYS // PACK16, PACK16, LANES), BF16),
                        pltpu.VMEM((PEER_HEADS, tm // LANES, N_KEYS, LANES), F32),
                        pltpu.VMEM((PEER_HEADS, tm // LANES, N_KEYS, LANES), F32),
                        pltpu.VMEM((d, tm), F32),
                        pltpu.VMEM((2, tm // LANES, MXU_DEPTH, LANES), F32),
                        pltpu.VMEM((tm // LANES, MXU_DEPTH, LANES), BF16),
                        pltpu.VMEM((tm // LANES, MXU_DEPTH, LANES), BF16)],
        compiler_params=_cparams(("arbitrary", "arbitrary")),
        name="peer",
    )(h2t, wq_t, k1, k2, down, up_t, x1, g2, norm_f)


def _rope_tables(rows, cols):
    r, c = jnp.meshgrid(jnp.arange(rows, dtype=F32), jnp.arange(cols, dtype=F32), indexing="ij")
    n_freq = RET_DK // 4
    inv = ROPE_BASE ** (-jnp.arange(n_freq, dtype=F32) / n_freq)
    ang = jnp.concatenate([r.reshape(-1, 1) * inv, c.reshape(-1, 1) * inv], axis=-1)
    cos, sin = jnp.cos(ang), jnp.sin(ang)
    return jnp.concatenate([cos, cos], axis=-1), jnp.concatenate([-sin, sin], axis=-1)


def kernel(x, c, ctx, c_ctx, w_mod, b_mod, norm1_w, norm2_w, w_in, ret_decay_f, ret_decay_b,
           gla_gk_up_f, gla_gk_bias_f, gla_gk_up_b, gla_gk_bias_b, w_out, peer_w_q, peer_k1, peer_k2,
           peer_down, peer_up, norm_f_w):
    depth = w_mod.shape[0]
    assert depth == 1
    b, l, d = x.shape
    lc = ctx.shape[1]
    li = 0

    cc = jnp.zeros((8, d), F32).at[:b].set(c).at[b].set(c_ctx)
    mod = _modulation(cc, w_mod[li], b_mod[li][None, :])
    sh1, sc1, g1, sh2, sc2, g2 = [mod[:b, j * d:(j + 1) * d][:, None, :] for j in range(N_MOD)]
    csh1, csc1 = [jnp.broadcast_to(mod[b:b + 1, j * d:(j + 1) * d][:, None, :], (b, 1, d)) for j in range(2)]

    w_main = w_in[li][:, :MAIN_COLS].astype(BF16)
    w_lr = jnp.zeros((d, LANES), F32).at[:, :2 * GLA_RANK].set(w_in[li][:, MAIN_COLS:])
    gk_up = jnp.zeros((LANES, 2 * GLA_KW), F32)
    gk_up = gk_up.at[:GLA_RANK, :GLA_KW].set(gla_gk_up_f[li])
    gk_up = gk_up.at[GLA_RANK:2 * GLA_RANK, GLA_KW:].set(gla_gk_up_b[li])
    gk_bias = jnp.concatenate([gla_gk_bias_f[li], gla_gk_bias_b[li]])[None, :]
    n1 = norm1_w[li][None, :]
    dec_f = jnp.zeros((8, LANES), F32).at[:RET_HEADS].set(jnp.broadcast_to(ret_decay_f[li][:, None], (RET_HEADS, LANES)))
    dec_b = jnp.zeros((8, LANES), F32).at[:RET_HEADS].set(jnp.broadcast_to(ret_decay_b[li][:, None], (RET_HEADS, LANES)))
    cos, sin = _rope_tables(l // GRID_W, GRID_W)
    cos_c = jnp.ones((lc, RET_DK), F32)
    sin_c = jnp.zeros((lc, RET_DK), F32)

    fc = _inproj(ctx, csh1, csc1, n1, w_main, w_lr, gk_up, gk_bias, cos_c, sin_c, tm=256)
    zr = jnp.zeros((b, RET_HEADS, RET_DK, RET_DK), F32)
    zg = jnp.zeros((b, GLA_HEADS, GLA_DK, GLA_DV), F32)
    ctx_out = _scan((fc[0], fc[1], fc[2], fc[4], fc[5], fc[6], fc[8], fc[9]), dec_f, dec_b,
                    (zr, zr, zg, zg), tb=256)
    ctx_states = ctx_out[4:]

    fl = _inproj(x, sh1, sc1, n1, w_main, w_lr, gk_up, gk_bias, cos, sin, tm=256)
    lat = _scan((fl[0], fl[1], fl[2], fl[4], fl[5], fl[6], fl[8], fl[9]), dec_f, dec_b,
                ctx_states, tb=256)

    x1, h2t = _mix(lat[0], lat[1], lat[2], lat[3], fl[3], fl[7], x, g1, sh2, sc2,
                   norm2_w[li][None, :], w_out[li].astype(BF16), tm=256)

    wq_t = peer_w_q[li].T.astype(BF16)
    return _peer(h2t, wq_t, peer_k1[li].astype(BF16), peer_k2[li].astype(BF16),
                 peer_down[li].astype(BF16), peer_up[li].T.astype(BF16), x1, g2, norm_f_w[None, :],
                 tm=512, ib=16)
```

```python
import functools

import jax
import jax.numpy as jnp
from jax import lax
from jax.experimental import pallas as pl
from jax.experimental.pallas import tpu as pltpu

F32 = jnp.float32
BF16 = jnp.bfloat16
HIGHEST = lax.Precision.HIGHEST

D_MODEL = 1024
GRID_W = 64
RET_HEADS = 4
RET_WIDTH = 512
RET_DK = 128
GLA_HEADS = 4
GLA_WIDTH = 512
GLA_DV = 128
GLA_DK = 64
GLA_KW = GLA_HEADS * GLA_DK
GLA_RANK = 16
GLA_GATE_NORM = 16.0
CHUNK = 64
ROPE_BASE = 10000.0
N_KEYS = 128
PEER_HEADS = 8
PEER_TOPK = 16
PEER_DQ = 256
N_MOD = 6
EPS = 1e-6
MAIN_COLS = 3584
LANES = 128
SUBLANES = 8
PACK16 = 16
MXU_DEPTH = 256
VMEM_LIMIT = 56 * 1024 * 1024


def _cparams(sem):
    return pltpu.CompilerParams(dimension_semantics=sem, vmem_limit_bytes=VMEM_LIMIT)


def _silu(v):
    return v / (1.0 + jnp.exp(-v))


def _log_sigmoid(z):
    return jnp.minimum(z, 0.0) - jnp.log(1.0 + jnp.exp(-jnp.abs(z)))


def _dot(a, b):
    return jnp.dot(a, b, preferred_element_type=F32)


def _dot_nt(a, b):
    return lax.dot_general(a, b, (((1,), (1,)), ((), ())), preferred_element_type=F32)


def _mod_kernel(c_ref, w_ref, b_ref, o_ref):
    a = _silu(c_ref[...])
    o_ref[...] = jnp.dot(a, w_ref[...], precision=HIGHEST, preferred_element_type=F32) + b_ref[...]


def _modulation(cc, w_mod, b_mod):
    n = w_mod.shape[1]
    tn = 1536
    return pl.pallas_call(
        _mod_kernel,
        grid=(n // tn,),
        in_specs=[pl.BlockSpec((8, D_MODEL), lambda j: (0, 0)),
                  pl.BlockSpec((D_MODEL, tn), lambda j: (0, j)),
                  pl.BlockSpec((1, tn), lambda j: (0, j))],
        out_specs=pl.BlockSpec((8, tn), lambda j: (0, j)),
        out_shape=jax.ShapeDtypeStruct((8, n), F32),
        compiler_params=_cparams(("arbitrary",)),
        name="modulation",
    )(cc, w_mod, b_mod)


def _inproj_kernel(x_ref, sh_ref, sc_ref, nw_ref, wm_ref, wl_ref, up_ref, gb_ref, cos_ref, sin_ref,
                   rq_ref, rk_ref, rv_ref, rg_ref, gq_ref, gk_ref, gv_ref, gg_ref, laf_ref, lab_ref):
    x = x_ref[0]
    ms = jnp.mean(x * x, axis=-1, keepdims=True)
    h = x * lax.rsqrt(ms + EPS) * nw_ref[...]
    h = h * (1.0 + sc_ref[0]) + sh_ref[0]
    p = _dot(h.astype(BF16), wm_ref[...])
    cos = cos_ref[...]
    sin = sin_ref[...]
    k_scale = RET_DK ** -0.5
    for hh in range(RET_HEADS):
        lo = hh * RET_DK
        q = p[:, lo:lo + RET_DK]
        k = p[:, RET_WIDTH + lo:RET_WIDTH + lo + RET_DK] * k_scale
        rq_ref[0, :, lo:lo + RET_DK] = q * cos + pltpu.roll(q, RET_DK // 2, axis=1) * sin
        rk_ref[0, :, lo:lo + RET_DK] = k * cos + pltpu.roll(k, RET_DK // 2, axis=1) * sin
    rv_ref[0] = p[:, 1024:1536]
    rg_ref[0] = p[:, 1536:2048]
    gq_ref[0] = p[:, 2048:2304] * (GLA_DK ** -0.5)
    gk_ref[0] = p[:, 2304:2560]
    gv_ref[0] = p[:, 2560:3072]
    gg_ref[0] = p[:, 3072:3584]
    lr = jnp.dot(h, wl_ref[...], precision=HIGHEST, preferred_element_type=F32)
    pre = jnp.dot(lr, up_ref[...], precision=HIGHEST, preferred_element_type=F32) + gb_ref[...]
    la = _log_sigmoid(pre) * (1.0 / GLA_GATE_NORM)
    laf_ref[0] = la[:, :GLA_KW]
    lab_ref[0] = la[:, GLA_KW:]


def _inproj(x, shift, scale, norm_w, w_main, w_lr, gk_up, gk_bias, cos, sin, tm):
    b, l, d = x.shape
    nt = l // tm
    tok = lambda w: pl.BlockSpec((1, tm, w), lambda bi, i: (bi, i, 0))
    full = lambda shp: pl.BlockSpec(shp, lambda bi, i: tuple(0 for _ in shp))
    vec = pl.BlockSpec((1, 1, d), lambda bi, i: (bi, 0, 0))
    widths = (RET_WIDTH, RET_WIDTH, RET_WIDTH, RET_WIDTH, GLA_KW, GLA_KW, GLA_WIDTH, GLA_WIDTH, GLA_KW, GLA_KW)
    return pl.pallas_call(
        _inproj_kernel,
        grid=(b, nt),
        in_specs=[tok(d), vec, vec, full((1, d)), full((d, MAIN_COLS)), full((d, LANES)),
                  full((LANES, 2 * GLA_KW)), full((1, 2 * GLA_KW)),
                  pl.BlockSpec((tm, RET_DK), lambda bi, i: (i, 0)),
                  pl.BlockSpec((tm, RET_DK), lambda bi, i: (i, 0))],
        out_specs=[tok(w) for w in widths],
        out_shape=[jax.ShapeDtypeStruct((b, l, w), F32) for w in widths],
        compiler_params=_cparams(("arbitrary", "arbitrary")),
        name="inproj",
    )(x, shift, scale, norm_w, w_main, w_lr, gk_up, gk_bias, cos, sin)


def _scan_kernel(decf_ref, decb_ref,
                 rqf_ref, rkf_ref, rvf_ref, rqb_ref, rkb_ref, rvb_ref,
                 gqf_ref, gkf_ref, gvf_ref, laf_ref, gqb_ref, gkb_ref, gvb_ref, lab_ref,
                 s0rf_ref, s0rb_ref, s0gf_ref, s0gb_ref,
                 orf_ref, orb_ref, ogf_ref, ogb_ref,
                 srf_out, srb_out, sgf_out, sgb_out,
                 srf, srb, sgf, sgb, *, nch):
    i = pl.program_id(1)

    @pl.when(i == 0)
    def _():
        srf[...] = s0rf_ref[0]
        srb[...] = s0rb_ref[0]
        sgf[...] = s0gf_ref[0]
        sgb[...] = s0gb_ref[0]

    C = CHUNK
    row = lax.broadcasted_iota(jnp.int32, (C, C), 0)
    col = lax.broadcasted_iota(jnp.int32, (C, C), 1)
    rel_f = (row - col).astype(F32)
    rel_b = (col - row).astype(F32)
    causal = row >= col
    anti = col >= row
    tri_lo = jnp.where(causal, 1.0, 0.0).astype(F32)
    tri_up = jnp.where(anti, 1.0, 0.0).astype(F32)
    pos_col = lax.broadcasted_iota(jnp.int32, (C, LANES), 0).astype(F32)

    lgf = _log_sigmoid(decf_ref[...])
    lgb = _log_sigmoid(decb_ref[...])

    def ret_consts(lg, h, fwd):
        lrow = lg[h:h + 1, :]
        rel = rel_f if fwd else rel_b
        mask = jnp.where(rel >= 0, jnp.exp(lrow[:, :C] * jnp.maximum(rel, 0.0)), 0.0)
        if fwd:
            qd = jnp.exp(lrow * (pos_col + 1.0))
            kd = jnp.exp(lrow * (C - 1.0 - pos_col))
        else:
            qd = jnp.exp(lrow * (C - pos_col))
            kd = jnp.exp(lrow * pos_col)
        cd = jnp.exp(lrow * float(C))
        return mask, qd, kd, cd

    cf = [ret_consts(lgf, h, True) for h in range(RET_HEADS)]
    cb = [ret_consts(lgb, h, False) for h in range(RET_HEADS)]

    def ret_chunk(q_ref, k_ref, v_ref, o_ref, s_ref, consts, r0):
        for h in range(RET_HEADS):
            mask, qd, kd, cd = consts[h]
            lo = h * RET_DK
            q = q_ref[0, r0:r0 + C, lo:lo + RET_DK]
            k = k_ref[0, r0:r0 + C, lo:lo + RET_DK]
            v = v_ref[0, r0:r0 + C, lo:lo + RET_DK].astype(BF16)
            s = s_ref[h]
            scores = _dot_nt(q.astype(BF16), k.astype(BF16)) * mask
            o = _dot(scores.astype(BF16), v) + _dot((q * qd).astype(BF16), s.astype(BF16))
            o_ref[0, r0:r0 + C, lo:lo + RET_DK] = o
            s_ref[h] = cd * s + _dot((k * kd).T.astype(BF16), v)

    def gla_chunk(q_ref, k_ref, v_ref, a_ref, o_ref, s_ref, fwd, r0):
        a = a_ref[0, r0:r0 + C, :]
        tri = tri_lo if fwd else tri_up
        g = jnp.dot(tri, a, precision=HIGHEST, preferred_element_type=F32)
        g_last = g[C - 1:C, :] if fwd else g[0:1, :]
        eg = jnp.exp(g)
        qt = q_ref[0, r0:r0 + C, :] * eg
        kk = k_ref[0, r0:r0 + C, :]
        kt = kk * jnp.exp(-g)
        kl = kk * jnp.exp(g_last - g)
        tot_col = jnp.exp(jnp.sum(a.T, axis=1, keepdims=True))
        keep = causal if fwd else anti
        for h in range(GLA_HEADS):
            lo = h * GLA_DK
            vo = h * GLA_DV
            v = v_ref[0, r0:r0 + C, vo:vo + GLA_DV].astype(BF16)
            s = s_ref[h]
            qh = qt[:, lo:lo + GLA_DK].astype(BF16)
            scores = jnp.where(keep, _dot_nt(qh, kt[:, lo:lo + GLA_DK].astype(BF16)), 0.0)
            o = _dot(scores.astype(BF16), v) + _dot(qh, s.astype(BF16))
            o_ref[0, r0:r0 + C, vo:vo + GLA_DV] = o
            s_ref[h] = tot_col[lo:lo + GLA_DK, :] * s + _dot(kl[:, lo:lo + GLA_DK].T.astype(BF16), v)

    for c in range(nch):
        rf = c * C
        rb = (nch - 1 - c) * C
        ret_chunk(rqf_ref, rkf_ref, rvf_ref, orf_ref, srf, cf, rf)
        ret_chunk(rqb_ref, rkb_ref, rvb_ref, orb_ref, srb, cb, rb)
        gla_chunk(gqf_ref, gkf_ref, gvf_ref, laf_ref, ogf_ref, sgf, True, rf)
        gla_chunk(gqb_ref, gkb_ref, gvb_ref, lab_ref, ogb_ref, sgb, False, rb)

    @pl.when(i == pl.num_programs(1) - 1)
    def _():
        srf_out[0] = srf[...]
        srb_out[0] = srb[...]
        sgf_out[0] = sgf[...]
        sgb_out[0] = sgb[...]


def _scan(feats, dec_f, dec_b, states, tb):
    rq, rk, rv, gq, gk, gv, la_f, la_b = feats
    b, l, _ = rq.shape
    nb = l // tb
    fwd = lambda w: pl.BlockSpec((1, tb, w), lambda bi, i: (bi, i, 0))
    bwd = lambda w: pl.BlockSpec((1, tb, w), lambda bi, i: (bi, nb - 1 - i, 0))
    dec = pl.BlockSpec((8, LANES), lambda bi, i: (0, 0))
    st_r = pl.BlockSpec((1, RET_HEADS, RET_DK, RET_DK), lambda bi, i: (bi, 0, 0, 0))
    st_g = pl.BlockSpec((1, GLA_HEADS, GLA_DK, GLA_DV), lambda bi, i: (bi, 0, 0, 0))
    o_shape = jax.ShapeDtypeStruct((b, l, RET_WIDTH), F32)
    sr_shape = jax.ShapeDtypeStruct((b, RET_HEADS, RET_DK, RET_DK), F32)
    sg_shape = jax.ShapeDtypeStruct((b, GLA_HEADS, GLA_DK, GLA_DV), F32)
    return pl.pallas_call(
        functools.partial(_scan_kernel, nch=tb // CHUNK),
        grid=(b, nb),
        in_specs=[dec, dec,
                  fwd(RET_WIDTH), fwd(RET_WIDTH), fwd(RET_WIDTH),
                  bwd(RET_WIDTH), bwd(RET_WIDTH), bwd(RET_WIDTH),
                  fwd(GLA_KW), fwd(GLA_KW), fwd(GLA_WIDTH), fwd(GLA_KW),
                  bwd(GLA_KW), bwd(GLA_KW), bwd(GLA_WIDTH), bwd(GLA_KW),
                  st_r, st_r, st_g, st_g],
        out_specs=[fwd(RET_WIDTH), bwd(RET_WIDTH), fwd(GLA_WIDTH), bwd(GLA_WIDTH),
                   st_r, st_r, st_g, st_g],
        out_shape=[o_shape, o_shape, o_shape, o_shape, sr_shape, sr_shape, sg_shape, sg_shape],
        scratch_shapes=[pltpu.VMEM((RET_HEADS, RET_DK, RET_DK), F32),
                        pltpu.VMEM((RET_HEADS, RET_DK, RET_DK), F32),
                        pltpu.VMEM((GLA_HEADS, GLA_DK, GLA_DV), F32),
                        pltpu.VMEM((GLA_HEADS, GLA_DK, GLA_DV), F32)],
        compiler_params=_cparams(("arbitrary", "arbitrary")),
        name="bidir_scan",
    )(dec_f, dec_b, rq, rk, rv, rq, rk, rv, gq, gk, gv, la_f, gq, gk, gv, la_b, *states)


def _mix_kernel(orf_ref, orb_ref, ogf_ref, ogb_ref, rg_ref, gg_ref, x_ref, g1_ref, sh_ref, sc_ref,
                nw_ref, wo_ref, x1_ref, h2t_ref):
    def head_norm(o):
        parts = []
        for h in range(RET_HEADS):
            oh = o[:, h * 128:(h + 1) * 128]
            parts.append(oh * lax.rsqrt(jnp.mean(oh * oh, axis=-1, keepdims=True) + EPS))
        return jnp.concatenate(parts, axis=-1)

    ret = head_norm(orf_ref[0] + orb_ref[0]) * _silu(rg_ref[0])
    gla = head_norm(ogf_ref[0] + ogb_ref[0]) * _silu(gg_ref[0])
    y = jnp.concatenate([ret, gla], axis=-1).astype(BF16)
    x1 = x_ref[0] + g1_ref[0] * _dot(y, wo_ref[...])
    x1_ref[0] = x1
    ms = jnp.mean(x1 * x1, axis=-1, keepdims=True)
    h2 = x1 * lax.rsqrt(ms + EPS) * nw_ref[...]
    h2 = h2 * (1.0 + sc_ref[0]) + sh_ref[0]
    h2t_ref[...] = h2.T.astype(BF16)


def _mix(o_rf, o_rb, o_gf, o_gb, rg, gg, x, g1, sh2, sc2, norm_w, w_out, tm):
    b, l, d = x.shape
    nt = l // tm
    tok = lambda w: pl.BlockSpec((1, tm, w), lambda bi, i: (bi, i, 0))
    vec = pl.BlockSpec((1, 1, d), lambda bi, i: (bi, 0, 0))
    return pl.pallas_call(
        _mix_kernel,
        grid=(b, nt),
        in_specs=[tok(512)] * 6 + [tok(d), vec, vec, vec,
                                   pl.BlockSpec((1, d), lambda bi, i: (0, 0)),
                                   pl.BlockSpec((d, d), lambda bi, i: (0, 0))],
        out_specs=[tok(d), pl.BlockSpec((d, tm), lambda bi, i: (0, bi * nt + i))],
        out_shape=[jax.ShapeDtypeStruct((b, l, d), F32), jax.ShapeDtypeStruct((d, b * l), BF16)],
        compiler_params=_cparams(("arbitrary", "arbitrary")),
        name="mix_outproj",
    )(o_rf, o_rb, o_gf, o_gb, rg, gg, x, g1, sh2, sc2, norm_w, w_out)


def _oddeven_merge_sort_pairs(n):
    pairs = []
    p = 1
    while p < n:
        k = p
        while k >= 1:
            for j in range(k % p, n - k, 2 * k):
                for i in range(min(k, n - j - k)):
                    if (i + j) // (2 * p) == (i + j + k) // (2 * p):
                        pairs.append((i + j, i + j + k))
            k //= 2
        p *= 2
    return pairs


def _bitonic_merge_pairs(n):
    pairs = []
    d = n // 2
    while d >= 1:
        pairs.extend((i, i + d) for i in range(n) if (i & d) == 0)
        d //= 2
    return pairs


_SORT16 = _oddeven_merge_sort_pairs(PEER_TOPK)
_MERGE16 = _bitonic_merge_pairs(PEER_TOPK)


def _compare_exchange(x, pairs):
    for a, b in pairs:
        hi = jnp.maximum(x[a], x[b])
        lo = jnp.minimum(x[a], x[b])
        x[a] = hi
        x[b] = lo


def _sublane_allreduce(v, op):
    for sh in (4, 2, 1):
        v = op(v, pltpu.roll(v, sh, axis=0))
    return v


def _top16_desc(s):
    x = [s[SUBLANES * k:SUBLANES * (k + 1), :] for k in range(PEER_TOPK)]
    _compare_exchange(x, _SORT16)
    for sh in (4, 2, 1):
        c = [jnp.maximum(x[k], pltpu.roll(x[PEER_TOPK - 1 - k], sh, axis=0)) for k in range(PEER_TOPK)]
        _compare_exchange(c, _MERGE16)
        x = c
    return x


def _route_group(s1, s2):
    inf = jnp.inf
    v1 = _top16_desc(s1)
    v2 = _top16_desc(s2)
    sub = lax.broadcasted_iota(jnp.int32, v1[0].shape, 0)

    def pack_rows(v):
        out = v[0]
        for r in range(1, SUBLANES):
            out = jnp.where(sub == r, v[r], out)
        return out

    v1lo, v1hi = pack_rows(v1[:SUBLANES]), pack_rows(v1[SUBLANES:])
    v2lo, v2hi = pack_rows(v2[:SUBLANES]), pack_rows(v2[SUBLANES:])
    cands = [v1[0] + v2lo, v1[0] + v2hi] + [v1[a] + v2lo for a in range(1, SUBLANES)] + [v1hi + v2[0]]
    c = list(cands)
    tau = None
    for _ in range(PEER_TOPK):
        m = c[0]
        for ci in c[1:]:
            m = jnp.maximum(m, ci)
        tau = _sublane_allreduce(m, jnp.maximum)
        c = [jnp.where(ci == tau, -inf, ci) for ci in c]
    m12 = v1[0] + v2[0]
    zs = None
    for cd in cands:
        t = jnp.where(cd >= tau, jnp.exp(cd - m12), 0.0)
        zs = t if zs is None else zs + t
    zinv = 1.0 / _sublane_allreduce(zs, jnp.add)
    thetas = []
    for b in range(PEER_TOPK):
        lo = jnp.where(v1lo + v2[b] >= tau, v1lo, inf)
        hi = jnp.where(v1hi + v2[b] >= tau, v1hi, inf)
        thetas.append(_sublane_allreduce(jnp.minimum(lo, hi), jnp.minimum))
    cnt, e1, rank, e2 = [], [], [], []
    for k in range(N_KEYS // SUBLANES):
        x = s1[SUBLANES * k:SUBLANES * (k + 1), :]
        y = s2[SUBLANES * k:SUBLANES * (k + 1), :]
        ck = jnp.where(x >= thetas[0], 1.0, 0.0)
        rk = jnp.where(v2[0] > y, 1.0, 0.0)
        for b in range(1, PEER_TOPK):
            ck = ck + jnp.where(x >= thetas[b], 1.0, 0.0)
            rk = rk + jnp.where(v2[b] > y, 1.0, 0.0)
        cnt.append(ck)
        rank.append(rk)
        e1.append(jnp.exp(x - v1[0]) * zinv)
        e2.append(jnp.exp(y - v2[0]))
    return cnt, e1, rank, e2


def _peer_kernel(h2t_ref, wqt_ref, k1_ref, k2_ref, down_ref, upt_ref, x1_ref, g2_ref, nf_ref,
                 out_ref,
                 rank2_s, e2_s, cnt_s, e1_s, acc_s, *, ib):
    e = pl.program_id(1)
    tm = h2t_ref.shape[1]
    half = PEER_DQ // 2
    nk2 = N_KEYS // PACK16

    @pl.when(e == 0)
    def _():
        acc_s[...] = jnp.zeros_like(acc_s)

        def head_body(h, carry):
            ht = h2t_ref[...]
            row0 = pl.multiple_of(h * PEER_DQ, PEER_DQ)
            q = _dot(wqt_ref[pl.ds(row0, PEER_DQ), :], ht).astype(BF16)
            s1 = _dot(k1_ref[h], q[:half, :])
            s2 = _dot(k2_ref[h], q[half:, :])
            for g in range(tm // LANES):
                sl = slice(g * LANES, (g + 1) * LANES)
                cnt, e1, rank, e2 = _route_group(s1[:, sl], s2[:, sl])
                cnt_s[h, :, sl] = jnp.concatenate(cnt, axis=0)
                e1_s[h, :, sl] = jnp.concatenate(e1, axis=0)
                for k2 in range(nk2):
                    rank2_s[h, k2, :, sl] = jnp.concatenate(rank[2 * k2:2 * k2 + 2], axis=0).astype(BF16)
                    e2_s[h, k2, :, sl] = jnp.concatenate(e2[2 * k2:2 * k2 + 2], axis=0).astype(BF16)
            return carry

        lax.fori_loop(0, PEER_HEADS, head_body, 0)

    a_t = _dot(down_ref[...], h2t_ref[...])
    hs = []
    for ii in range(ib):
        i = e * ib + ii
        w = jnp.zeros((nk2, PACK16, tm), BF16)
        for h in range(PEER_HEADS):
            c_row = jnp.broadcast_to(cnt_s[h, pl.ds(i, 1), :], (PACK16, tm)).astype(BF16)
            e_row = jnp.broadcast_to(e1_s[h, pl.ds(i, 1), :], (PACK16, tm)).astype(BF16)
            w = w + jnp.where(rank2_s[h] < c_row[None], e2_s[h], jnp.zeros((), BF16)) * e_row[None]
        a = a_t[ii * N_KEYS:(ii + 1) * N_KEYS, :]
        act = 0.5 * a * (1.0 + lax.erf(a * (2.0 ** -0.5)))
        hs.append((w * act.astype(BF16).reshape(nk2, PACK16, tm)).reshape(N_KEYS, tm))
    acc_s[...] += _dot(upt_ref[...], jnp.concatenate(hs, axis=0))

    @pl.when(e == pl.num_programs(1) - 1)
    def _():
        y = x1_ref[0] + g2_ref[0] * acc_s[...].T
        ms = jnp.mean(y * y, axis=-1, keepdims=True)
        out_ref[0] = y * lax.rsqrt(ms + EPS) * nf_ref[...]


def _peer(h2t, wq_t, k1, k2, down, up_t, x1, g2, norm_f, tm, ib):
    b, l, d = x1.shape
    nt = l // tm
    ne = N_KEYS // ib
    te = ib * N_KEYS
    const = lambda shp: pl.BlockSpec(shp, lambda t, e: tuple(0 for _ in shp))
    return pl.pallas_call(
        functools.partial(_peer_kernel, ib=ib),
        grid=(b * nt, ne),
        in_specs=[pl.BlockSpec((d, tm), lambda t, e: (0, t)),
                  const((PEER_HEADS * PEER_DQ, d)),
                  const((PEER_HEADS, N_KEYS, PEER_DQ // 2)),
                  const((PEER_HEADS, N_KEYS, PEER_DQ // 2)),
                  pl.BlockSpec((te, d), lambda t, e: (e, 0)),
                  pl.BlockSpec((d, te), lambda t, e: (0, e)),
                  pl.BlockSpec((1, tm, d), lambda t, e: (t // nt, t % nt, 0)),
                  pl.BlockSpec((1, 1, d), lambda t, e: (t // nt, 0, 0)),
                  const((1, d))],
        out_specs=pl.BlockSpec((1, tm, d), lambda t, e: (t // nt, t % nt, 0)),
        out_shape=jax.ShapeDtypeStruct((b, l, d), F32),
        scratch_shapes=[pltpu.VMEM((PEER_HEADS, N_KEYS // PACK16, PACK16, tm), BF16),
                        pltpu.VMEM((PEER_HEADS, N_KEYS // PACK16, PACK16, tm), BF16),
                        pltpu.VMEM((PEER_HEADS, N_KEYS, tm), F32),
                        pltpu.VMEM((PEER_HEADS, N_KEYS, tm), F32),
                        pltpu.VMEM((d, tm), F32)],
        compiler_params=_cparams(("arbitrary", "arbitrary")),
        name="peer",
    )(h2t, wq_t, k1, k2, down, up_t, x1, g2, norm_f)


def _rope_tables(rows, cols):
    r, c = jnp.meshgrid(jnp.arange(rows, dtype=F32), jnp.arange(cols, dtype=F32), indexing="ij")
    n_freq = RET_DK // 4
    inv = ROPE_BASE ** (-jnp.arange(n_freq, dtype=F32) / n_freq)
    ang = jnp.concatenate([r.reshape(-1, 1) * inv, c.reshape(-1, 1) * inv], axis=-1)
    cos, sin = jnp.cos(ang), jnp.sin(ang)
    return jnp.concatenate([cos, cos], axis=-1), jnp.concatenate([-sin, sin], axis=-1)


def kernel(x, c, ctx, c_ctx, w_mod, b_mod, norm1_w, norm2_w, w_in, ret_decay_f, ret_decay_b,
           gla_gk_up_f, gla_gk_bias_f, gla_gk_up_b, gla_gk_bias_b, w_out, peer_w_q, peer_k1, peer_k2,
           peer_down, peer_up, norm_f_w):
    depth = w_mod.shape[0]
    assert depth == 1
    b, l, d = x.shape
    lc = ctx.shape[1]
    li = 0

    cc = jnp.zeros((8, d), F32).at[:b].set(c).at[b].set(c_ctx)
    mod = _modulation(cc, w_mod[li], b_mod[li][None, :])
    sh1, sc1, g1, sh2, sc2, g2 = [mod[:b, j * d:(j + 1) * d][:, None, :] for j in range(N_MOD)]
    csh1, csc1 = [jnp.broadcast_to(mod[b:b + 1, j * d:(j + 1) * d][:, None, :], (b, 1, d)) for j in range(2)]

    w_main = w_in[li][:, :MAIN_COLS].astype(BF16)
    w_lr = jnp.zeros((d, LANES), F32).at[:, :2 * GLA_RANK].set(w_in[li][:, MAIN_COLS:])
    gk_up = jnp.zeros((LANES, 2 * GLA_KW), F32)
    gk_up = gk_up.at[:GLA_RANK, :GLA_KW].set(gla_gk_up_f[li])
    gk_up = gk_up.at[GLA_RANK:2 * GLA_RANK, GLA_KW:].set(gla_gk_up_b[li])
    gk_bias = jnp.concatenate([gla_gk_bias_f[li], gla_gk_bias_b[li]])[None, :]
    n1 = norm1_w[li][None, :]
    dec_f = jnp.zeros((8, LANES), F32).at[:RET_HEADS].set(jnp.broadcast_to(ret_decay_f[li][:, None], (RET_HEADS, LANES)))
    dec_b = jnp.zeros((8, LANES), F32).at[:RET_HEADS].set(jnp.broadcast_to(ret_decay_b[li][:, None], (RET_HEADS, LANES)))
    cos, sin = _rope_tables(l // GRID_W, GRID_W)
    cos_c = jnp.ones((lc, RET_DK), F32)
    sin_c = jnp.zeros((lc, RET_DK), F32)

    fc = _inproj(ctx, csh1, csc1, n1, w_main, w_lr, gk_up, gk_bias, cos_c, sin_c, tm=256)
    zr = jnp.zeros((b, RET_HEADS, RET_DK, RET_DK), F32)
    zg = jnp.zeros((b, GLA_HEADS, GLA_DK, GLA_DV), F32)
    ctx_out = _scan((fc[0], fc[1], fc[2], fc[4], fc[5], fc[6], fc[8], fc[9]), dec_f, dec_b,
                    (zr, zr, zg, zg), tb=256)
    ctx_states = ctx_out[4:]

    fl = _inproj(x, sh1, sc1, n1, w_main, w_lr, gk_up, gk_bias, cos, sin, tm=256)
    lat = _scan((fl[0], fl[1], fl[2], fl[4], fl[5], fl[6], fl[8], fl[9]), dec_f, dec_b,
                ctx_states, tb=256)

    x1, h2t = _mix(lat[0], lat[1], lat[2], lat[3], fl[3], fl[7], x, g1, sh2, sc2,
                   norm2_w[li][None, :], w_out[li].astype(BF16), tm=256)

    wq_t = peer_w_q[li].T.astype(BF16)
    return _peer(h2t, wq_t, peer_k1[li].astype(BF16), peer_k2[li].astype(BF16),
                 peer_down[li].astype(BF16), peer_up[li].T.astype(BF16), x1, g2, norm_f_w[None, :],
                 tm=512, ib=8)
```

```python
import functools

import jax
import jax.numpy as jnp
from jax import lax
from jax.experimental import pallas as pl
from jax.experimental.pallas import tpu as pltpu

F32 = jnp.float32
BF16 = jnp.bfloat16
HIGHEST = lax.Precision.HIGHEST

D_MODEL = 1024
GRID_W = 64
RET_HEADS = 4
RET_WIDTH = 512
RET_DK = 128
GLA_HEADS = 4
GLA_WIDTH = 512
GLA_DV = 128
GLA_DK = 64
GLA_KW = GLA_HEADS * GLA_DK
GLA_RANK = 16
GLA_GATE_NORM = 16.0
CHUNK = 64
ROPE_BASE = 10000.0
N_KEYS = 128
PEER_HEADS = 8
PEER_TOPK = 16
PEER_DQ = 256
N_MOD = 6
EPS = 1e-6
MAIN_COLS = 3584
LANES = 128
SUBLANES = 8
PACK16 = 16
MXU_DEPTH = 256
VMEM_LIMIT = 56 * 1024 * 1024


def _cparams(sem):
    return pltpu.CompilerParams(dimension_semantics=sem, vmem_limit_bytes=VMEM_LIMIT)


def _silu(v):
    return v / (1.0 + jnp.exp(-v))


def _log_sigmoid(z):
    return jnp.minimum(z, 0.0) - jnp.log(1.0 + jnp.exp(-jnp.abs(z)))


def _dot(a, b):
    return jnp.dot(a, b, preferred_element_type=F32)


def _dot_nt(a, b):
    return lax.dot_general(a, b, (((1,), (1,)), ((), ())), preferred_element_type=F32)


def _mod_kernel(c_ref, w_ref, b_ref, o_ref):
    a = _silu(c_ref[...])
    o_ref[...] = jnp.dot(a, w_ref[...], precision=HIGHEST, preferred_element_type=F32) + b_ref[...]


def _modulation(cc, w_mod, b_mod):
    n = w_mod.shape[1]
    tn = 1536
    return pl.pallas_call(
        _mod_kernel,
        grid=(n // tn,),
        in_specs=[pl.BlockSpec((8, D_MODEL), lambda j: (0, 0)),
                  pl.BlockSpec((D_MODEL, tn), lambda j: (0, j)),
                  pl.BlockSpec((1, tn), lambda j: (0, j))],
        out_specs=pl.BlockSpec((8, tn), lambda j: (0, j)),
        out_shape=jax.ShapeDtypeStruct((8, n), F32),
        compiler_params=_cparams(("arbitrary",)),
        name="modulation",
    )(cc, w_mod, b_mod)


def _inproj_kernel(x_ref, sh_ref, sc_ref, nw_ref, wm_ref, wl_ref, up_ref, gb_ref, cos_ref, sin_ref,
                   rq_ref, rk_ref, rv_ref, rg_ref, gq_ref, gk_ref, gv_ref, gg_ref, laf_ref, lab_ref):
    x = x_ref[0]
    ms = jnp.mean(x * x, axis=-1, keepdims=True)
    h = x * lax.rsqrt(ms + EPS) * nw_ref[...]
    h = h * (1.0 + sc_ref[0]) + sh_ref[0]
    p = _dot(h.astype(BF16), wm_ref[...])
    cos = cos_ref[...]
    sin = sin_ref[...]
    k_scale = RET_DK ** -0.5
    for hh in range(RET_HEADS):
        lo = hh * RET_DK
        q = p[:, lo:lo + RET_DK]
        k = p[:, RET_WIDTH + lo:RET_WIDTH + lo + RET_DK] * k_scale
        rq_ref[0, :, lo:lo + RET_DK] = q * cos + pltpu.roll(q, RET_DK // 2, axis=1) * sin
        rk_ref[0, :, lo:lo + RET_DK] = k * cos + pltpu.roll(k, RET_DK // 2, axis=1) * sin
    rv_ref[0] = p[:, 1024:1536]
    rg_ref[0] = p[:, 1536:2048]
    gq_ref[0] = p[:, 2048:2304] * (GLA_DK ** -0.5)
    gk_ref[0] = p[:, 2304:2560]
    gv_ref[0] = p[:, 2560:3072]
    gg_ref[0] = p[:, 3072:3584]
    lr = jnp.dot(h, wl_ref[...], precision=HIGHEST, preferred_element_type=F32)
    pre = jnp.dot(lr, up_ref[...], precision=HIGHEST, preferred_element_type=F32) + gb_ref[...]
    la = _log_sigmoid(pre) * (1.0 / GLA_GATE_NORM)
    laf_ref[0] = la[:, :GLA_KW]
    lab_ref[0] = la[:, GLA_KW:]


def _inproj(x, shift, scale, norm_w, w_main, w_lr, gk_up, gk_bias, cos, sin, tm):
    b, l, d = x.shape
    nt = l // tm
    tok = lambda w: pl.BlockSpec((1, tm, w), lambda bi, i: (bi, i, 0))
    full = lambda shp: pl.BlockSpec(shp, lambda bi, i: tuple(0 for _ in shp))
    vec = pl.BlockSpec((1, 1, d), lambda bi, i: (bi, 0, 0))
    widths = (RET_WIDTH, RET_WIDTH, RET_WIDTH, RET_WIDTH, GLA_KW, GLA_KW, GLA_WIDTH, GLA_WIDTH, GLA_KW, GLA_KW)
    return pl.pallas_call(
        _inproj_kernel,
        grid=(b, nt),
        in_specs=[tok(d), vec, vec, full((1, d)), full((d, MAIN_COLS)), full((d, LANES)),
                  full((LANES, 2 * GLA_KW)), full((1, 2 * GLA_KW)),
                  pl.BlockSpec((tm, RET_DK), lambda bi, i: (i, 0)),
                  pl.BlockSpec((tm, RET_DK), lambda bi, i: (i, 0))],
        out_specs=[tok(w) for w in widths],
        out_shape=[jax.ShapeDtypeStruct((b, l, w), F32) for w in widths],
        compiler_params=_cparams(("arbitrary", "arbitrary")),
        name="inproj",
    )(x, shift, scale, norm_w, w_main, w_lr, gk_up, gk_bias, cos, sin)


def _scan_kernel(decf_ref, decb_ref,
                 rqf_ref, rkf_ref, rvf_ref, rqb_ref, rkb_ref, rvb_ref,
                 gqf_ref, gkf_ref, gvf_ref, laf_ref, gqb_ref, gkb_ref, gvb_ref, lab_ref,
                 s0rf_ref, s0rb_ref, s0gf_ref, s0gb_ref,
                 orf_ref, orb_ref, ogf_ref, ogb_ref,
                 srf_out, srb_out, sgf_out, sgb_out,
                 srf, srb, sgf, sgb, *, nch):
    i = pl.program_id(1)

    @pl.when(i == 0)
    def _():
        srf[...] = s0rf_ref[0]
        srb[...] = s0rb_ref[0]
        sgf[...] = s0gf_ref[0]
        sgb[...] = s0gb_ref[0]

    C = CHUNK
    row = lax.broadcasted_iota(jnp.int32, (C, C), 0)
    col = lax.broadcasted_iota(jnp.int32, (C, C), 1)
    rel_f = (row - col).astype(F32)
    rel_b = (col - row).astype(F32)
    causal = row >= col
    anti = col >= row
    tri_lo = jnp.where(causal, 1.0, 0.0).astype(F32)
    tri_up = jnp.where(anti, 1.0, 0.0).astype(F32)
    pos_col = lax.broadcasted_iota(jnp.int32, (C, LANES), 0).astype(F32)

    lgf = _log_sigmoid(decf_ref[...])
    lgb = _log_sigmoid(decb_ref[...])

    def ret_consts(lg, h, fwd):
        lrow = lg[h:h + 1, :]
        rel = rel_f if fwd else rel_b
        mask = jnp.where(rel >= 0, jnp.exp(lrow[:, :C] * jnp.maximum(rel, 0.0)), 0.0)
        if fwd:
            qd = jnp.exp(lrow * (pos_col + 1.0))
            kd = jnp.exp(lrow * (C - 1.0 - pos_col))
        else:
            qd = jnp.exp(lrow * (C - pos_col))
            kd = jnp.exp(lrow * pos_col)
        cd = jnp.exp(lrow * float(C))
        return mask, qd, kd, cd

    cf = [ret_consts(lgf, h, True) for h in range(RET_HEADS)]
    cb = [ret_consts(lgb, h, False) for h in range(RET_HEADS)]

    def ret_chunk(q_ref, k_ref, v_ref, o_ref, s_ref, consts, r0):
        for h in range(RET_HEADS):
            mask, qd, kd, cd = consts[h]
            lo = h * RET_DK
            q = q_ref[0, r0:r0 + C, lo:lo + RET_DK]
            k = k_ref[0, r0:r0 + C, lo:lo + RET_DK]
            v = v_ref[0, r0:r0 + C, lo:lo + RET_DK].astype(BF16)
            s = s_ref[h]
            scores = _dot_nt(q.astype(BF16), k.astype(BF16)) * mask
            o = _dot(scores.astype(BF16), v) + _dot((q * qd).astype(BF16), s.astype(BF16))
            o_ref[0, r0:r0 + C, lo:lo + RET_DK] = o
            s_ref[h] = cd * s + _dot((k * kd).T.astype(BF16), v)

    def gla_chunk(q_ref, k_ref, v_ref, a_ref, o_ref, s_ref, fwd, r0):
        a = a_ref[0, r0:r0 + C, :]
        tri = tri_lo if fwd else tri_up
        g = jnp.dot(tri, a, precision=HIGHEST, preferred_element_type=F32)
        g_last = g[C - 1:C, :] if fwd else g[0:1, :]
        eg = jnp.exp(g)
        qt = q_ref[0, r0:r0 + C, :] * eg
        kk = k_ref[0, r0:r0 + C, :]
        kt = kk * jnp.exp(-g)
        kl = kk * jnp.exp(g_last - g)
        tot_col = jnp.exp(jnp.sum(a.T, axis=1, keepdims=True))
        keep = causal if fwd else anti
        for h in range(GLA_HEADS):
            lo = h * GLA_DK
            vo = h * GLA_DV
            v = v_ref[0, r0:r0 + C, vo:vo + GLA_DV].astype(BF16)
            s = s_ref[h]
            qh = qt[:, lo:lo + GLA_DK].astype(BF16)
            scores = jnp.where(keep, _dot_nt(qh, kt[:, lo:lo + GLA_DK].astype(BF16)), 0.0)
            o = _dot(scores.astype(BF16), v) + _dot(qh, s.astype(BF16))
            o_ref[0, r0:r0 + C, vo:vo + GLA_DV] = o
            s_ref[h] = tot_col[lo:lo + GLA_DK, :] * s + _dot(kl[:, lo:lo + GLA_DK].T.astype(BF16), v)

    for c in range(nch):
        rf = c * C
        rb = (nch - 1 - c) * C
        ret_chunk(rqf_ref, rkf_ref, rvf_ref, orf_ref, srf, cf, rf)
        ret_chunk(rqb_ref, rkb_ref, rvb_ref, orb_ref, srb, cb, rb)
        gla_chunk(gqf_ref, gkf_ref, gvf_ref, laf_ref, ogf_ref, sgf, True, rf)
        gla_chunk(gqb_ref, gkb_ref, gvb_ref, lab_ref, ogb_ref, sgb, False, rb)

    @pl.when(i == pl.num_programs(1) - 1)
    def _():
        srf_out[0] = srf[...]
        srb_out[0] = srb[...]
        sgf_out[0] = sgf[...]
        sgb_out[0] = sgb[...]


def _scan(feats, dec_f, dec_b, states, tb):
    rq, rk, rv, gq, gk, gv, la_f, la_b = feats
    b, l, _ = rq.shape
    nb = l // tb
    fwd = lambda w: pl.BlockSpec((1, tb, w), lambda bi, i: (bi, i, 0))
    bwd = lambda w: pl.BlockSpec((1, tb, w), lambda bi, i: (bi, nb - 1 - i, 0))
    dec = pl.BlockSpec((8, LANES), lambda bi, i: (0, 0))
    st_r = pl.BlockSpec((1, RET_HEADS, RET_DK, RET_DK), lambda bi, i: (bi, 0, 0, 0))
    st_g = pl.BlockSpec((1, GLA_HEADS, GLA_DK, GLA_DV), lambda bi, i: (bi, 0, 0, 0))
    o_shape = jax.ShapeDtypeStruct((b, l, RET_WIDTH), F32)
    sr_shape = jax.ShapeDtypeStruct((b, RET_HEADS, RET_DK, RET_DK), F32)
    sg_shape = jax.ShapeDtypeStruct((b, GLA_HEADS, GLA_DK, GLA_DV), F32)
    return pl.pallas_call(
        functools.partial(_scan_kernel, nch=tb // CHUNK),
        grid=(b, nb),
        in_specs=[dec, dec,
                  fwd(RET_WIDTH), fwd(RET_WIDTH), fwd(RET_WIDTH),
                  bwd(RET_WIDTH), bwd(RET_WIDTH), bwd(RET_WIDTH),
                  fwd(GLA_KW), fwd(GLA_KW), fwd(GLA_WIDTH), fwd(GLA_KW),
                  bwd(GLA_KW), bwd(GLA_KW), bwd(GLA_WIDTH), bwd(GLA_KW),
                  st_r, st_r, st_g, st_g],
        out_specs=[fwd(RET_WIDTH), bwd(RET_WIDTH), fwd(GLA_WIDTH), bwd(GLA_WIDTH),
                   st_r, st_r, st_g, st_g],
        out_shape=[o_shape, o_shape, o_shape, o_shape, sr_shape, sr_shape, sg_shape, sg_shape],
        scratch_shapes=[pltpu.VMEM((RET_HEADS, RET_DK, RET_DK), F32),
                        pltpu.VMEM((RET_HEADS, RET_DK, RET_DK), F32),
                        pltpu.VMEM((GLA_HEADS, GLA_DK, GLA_DV), F32),
                        pltpu.VMEM((GLA_HEADS, GLA_DK, GLA_DV), F32)],
        compiler_params=_cparams(("arbitrary", "arbitrary")),
        name="bidir_scan",
    )(dec_f, dec_b, rq, rk, rv, rq, rk, rv, gq, gk, gv, la_f, gq, gk, gv, la_b, *states)


def _mix_kernel(orf_ref, orb_ref, ogf_ref, ogb_ref, rg_ref, gg_ref, x_ref, g1_ref, sh_ref, sc_ref,
                nw_ref, wo_ref, x1_ref, h2t_ref):
    def head_norm(o):
        parts = []
        for h in range(RET_HEADS):
            oh = o[:, h * 128:(h + 1) * 128]
            parts.append(oh * lax.rsqrt(jnp.mean(oh * oh, axis=-1, keepdims=True) + EPS))
        return jnp.concatenate(parts, axis=-1)

    ret = head_norm(orf_ref[0] + orb_ref[0]) * _silu(rg_ref[0])
    gla = head_norm(ogf_ref[0] + ogb_ref[0]) * _silu(gg_ref[0])
    y = jnp.concatenate([ret, gla], axis=-1).astype(BF16)
    x1 = x_ref[0] + g1_ref[0] * _dot(y, wo_ref[...])
    x1_ref[0] = x1
    ms = jnp.mean(x1 * x1, axis=-1, keepdims=True)
    h2 = x1 * lax.rsqrt(ms + EPS) * nw_ref[...]
    h2 = h2 * (1.0 + sc_ref[0]) + sh_ref[0]
    h2t_ref[...] = h2.T.astype(BF16)


def _mix(o_rf, o_rb, o_gf, o_gb, rg, gg, x, g1, sh2, sc2, norm_w, w_out, tm):
    b, l, d = x.shape
    nt = l // tm
    tok = lambda w: pl.BlockSpec((1, tm, w), lambda bi, i: (bi, i, 0))
    vec = pl.BlockSpec((1, 1, d), lambda bi, i: (bi, 0, 0))
    return pl.pallas_call(
        _mix_kernel,
        grid=(b, nt),
        in_specs=[tok(512)] * 6 + [tok(d), vec, vec, vec,
                                   pl.BlockSpec((1, d), lambda bi, i: (0, 0)),
                                   pl.BlockSpec((d, d), lambda bi, i: (0, 0))],
        out_specs=[tok(d), pl.BlockSpec((d, tm), lambda bi, i: (0, bi * nt + i))],
        out_shape=[jax.ShapeDtypeStruct((b, l, d), F32), jax.ShapeDtypeStruct((d, b * l), BF16)],
        compiler_params=_cparams(("arbitrary", "arbitrary")),
        name="mix_outproj",
    )(o_rf, o_rb, o_gf, o_gb, rg, gg, x, g1, sh2, sc2, norm_w, w_out)


def _oddeven_merge_sort_pairs(n):
    pairs = []
    p = 1
    while p < n:
        k = p
        while k >= 1:
            for j in range(k % p, n - k, 2 * k):
                for i in range(min(k, n - j - k)):
                    if (i + j) // (2 * p) == (i + j + k) // (2 * p):
                        pairs.append((i + j, i + j + k))
            k //= 2
        p *= 2
    return pairs


def _bitonic_merge_pairs(n):
    pairs = []
    d = n // 2
    while d >= 1:
        pairs.extend((i, i + d) for i in range(n) if (i & d) == 0)
        d //= 2
    return pairs


_SORT16 = _oddeven_merge_sort_pairs(PEER_TOPK)
_MERGE16 = _bitonic_merge_pairs(PEER_TOPK)


def _compare_exchange(x, pairs):
    for a, b in pairs:
        hi = jnp.maximum(x[a], x[b])
        lo = jnp.minimum(x[a], x[b])
        x[a] = hi
        x[b] = lo


def _sublane_allreduce(v, op):
    for sh in (4, 2, 1):
        v = op(v, pltpu.roll(v, sh, axis=0))
    return v


def _top16_desc(s):
    x = [s[SUBLANES * k:SUBLANES * (k + 1), :] for k in range(PEER_TOPK)]
    _compare_exchange(x, _SORT16)
    for sh in (4, 2, 1):
        c = [jnp.maximum(x[k], pltpu.roll(x[PEER_TOPK - 1 - k], sh, axis=0)) for k in range(PEER_TOPK)]
        _compare_exchange(c, _MERGE16)
        x = c
    return x


def _route_group(s1, s2):
    inf = jnp.inf
    v1 = _top16_desc(s1)
    v2 = _top16_desc(s2)
    sub = lax.broadcasted_iota(jnp.int32, v1[0].shape, 0)

    def pack_rows(v):
        out = v[0]
        for r in range(1, SUBLANES):
            out = jnp.where(sub == r, v[r], out)
        return out

    v1lo, v1hi = pack_rows(v1[:SUBLANES]), pack_rows(v1[SUBLANES:])
    v2lo, v2hi = pack_rows(v2[:SUBLANES]), pack_rows(v2[SUBLANES:])
    cands = [v1[0] + v2lo, v1[0] + v2hi] + [v1[a] + v2lo for a in range(1, SUBLANES)] + [v1hi + v2[0]]
    c = list(cands)
    tau = None
    for _ in range(PEER_TOPK):
        m = c[0]
        for ci in c[1:]:
            m = jnp.maximum(m, ci)
        tau = _sublane_allreduce(m, jnp.maximum)
        c = [jnp.where(ci == tau, -inf, ci) for ci in c]
    m12 = v1[0] + v2[0]
    zs = None
    for cd in cands:
        t = jnp.where(cd >= tau, jnp.exp(cd - m12), 0.0)
        zs = t if zs is None else zs + t
    zinv = 1.0 / _sublane_allreduce(zs, jnp.add)
    thetas = []
    for b in range(PEER_TOPK):
        lo = jnp.where(v1lo + v2[b] >= tau, v1lo, inf)
        hi = jnp.where(v1hi + v2[b] >= tau, v1hi, inf)
        thetas.append(_sublane_allreduce(jnp.minimum(lo, hi), jnp.minimum))
    cnt, e1, rank, e2 = [], [], [], []
    for k in range(N_KEYS // SUBLANES):
        x = s1[SUBLANES * k:SUBLANES * (k + 1), :]
        y = s2[SUBLANES * k:SUBLANES * (k + 1), :]
        ck = jnp.where(x >= thetas[0], 1.0, 0.0)
        rk = jnp.where(v2[0] > y, 1.0, 0.0)
        for b in range(1, PEER_TOPK):
            ck = ck + jnp.where(x >= thetas[b], 1.0, 0.0)
            rk = rk + jnp.where(v2[b] > y, 1.0, 0.0)
        cnt.append(ck)
        rank.append(rk)
        e1.append(jnp.exp(x - v1[0]) * zinv)
        e2.append(jnp.exp(y - v2[0]))
    return cnt, e1, rank, e2


def _peer_kernel(h2t_ref, wqt_ref, k1_ref, k2_ref, down_ref, upt_ref, x1_ref, g2_ref, nf_ref,
                 out_ref,
                 rank2_s, e2_s, cnt_s, e1_s, acc_s, *, ib):
    e = pl.program_id(1)
    tm = h2t_ref.shape[1]
    half = PEER_DQ // 2
    nk2 = N_KEYS // PACK16

    @pl.when(e == 0)
    def _():
        acc_s[...] = jnp.zeros_like(acc_s)

        def head_body(h, carry):
            ht = h2t_ref[...]
            row0 = pl.multiple_of(h * PEER_DQ, PEER_DQ)
            q = _dot(wqt_ref[pl.ds(row0, PEER_DQ), :], ht).astype(BF16)
            s1 = _dot(k1_ref[h], q[:half, :])
            s2 = _dot(k2_ref[h], q[half:, :])
            for g in range(tm // LANES):
                sl = slice(g * LANES, (g + 1) * LANES)
                cnt, e1, rank, e2 = _route_group(s1[:, sl], s2[:, sl])
                cnt_s[h, :, sl] = jnp.concatenate(cnt, axis=0)
                e1_s[h, :, sl] = jnp.concatenate(e1, axis=0)
                for k2 in range(nk2):
                    rank2_s[h, k2, :, sl] = jnp.concatenate(rank[2 * k2:2 * k2 + 2], axis=0).astype(BF16)
                    e2_s[h, k2, :, sl] = jnp.concatenate(e2[2 * k2:2 * k2 + 2], axis=0).astype(BF16)
            return carry

        lax.fori_loop(0, PEER_HEADS, head_body, 0)

    a_t = _dot(down_ref[...], h2t_ref[...])
    hs = []
    for ii in range(ib):
        i = e * ib + ii
        w = jnp.zeros((nk2, PACK16, tm), BF16)
        for h in range(PEER_HEADS):
            c_row = jnp.broadcast_to(cnt_s[h, pl.ds(i, 1), :], (PACK16, tm)).astype(BF16)
            e_row = jnp.broadcast_to(e1_s[h, pl.ds(i, 1), :], (PACK16, tm)).astype(BF16)
            w = w + jnp.where(rank2_s[h] < c_row[None], e2_s[h], jnp.zeros((), BF16)) * e_row[None]
        a = a_t[ii * N_KEYS:(ii + 1) * N_KEYS, :]
        act = 0.5 * a * (1.0 + lax.erf(a * (2.0 ** -0.5)))
        hs.append((w * act.astype(BF16).reshape(nk2, PACK16, tm)).reshape(N_KEYS, tm))
    acc_s[...] += _dot(upt_ref[...], jnp.concatenate(hs, axis=0))

    @pl.when(e == pl.num_programs(1) - 1)
    def _():
        y = x1_ref[0] + g2_ref[0] * acc_s[...].T
        ms = jnp.mean(y * y, axis=-1, keepdims=True)
        out_ref[0] = y * lax.rsqrt(ms + EPS) * nf_ref[...]


def _peer(h2t, wq_t, k1, k2, down, up_t, x1, g2, norm_f, tm, ib):
    b, l, d = x1.shape
    nt = l // tm
    ne = N_KEYS // ib
    te = ib * N_KEYS
    const = lambda shp: pl.BlockSpec(shp, lambda t, e: tuple(0 for _ in shp))
    return pl.pallas_call(
        functools.partial(_peer_kernel, ib=ib),
        grid=(b * nt, ne),
        in_specs=[pl.BlockSpec((d, tm), lambda t, e: (0, t)),
                  const((PEER_HEADS * PEER_DQ, d)),
                  const((PEER_HEADS, N_KEYS, PEER_DQ // 2)),
                  const((PEER_HEADS, N_KEYS, PEER_DQ // 2)),
                  pl.BlockSpec((te, d), lambda t, e: (e, 0)),
                  pl.BlockSpec((d, te), lambda t, e: (0, e)),
                  pl.BlockSpec((1, tm, d), lambda t, e: (t // nt, t % nt, 0)),
                  pl.BlockSpec((1, 1, d), lambda t, e: (t // nt, 0, 0)),
                  const((1, d))],
        out_specs=pl.BlockSpec((1, tm, d), lambda t, e: (t // nt, t % nt, 0)),
        out_shape=jax.ShapeDtypeStruct((b, l, d), F32),
        scratch_shapes=[pltpu.VMEM((PEER_HEADS, N_KEYS // PACK16, PACK16, tm), BF16),
                        pltpu.VMEM((PEER_HEADS, N_KEYS // PACK16, PACK16, tm), BF16),
                        pltpu.VMEM((PEER_HEADS, N_KEYS, tm), F32),
                        pltpu.VMEM((PEER_HEADS, N_KEYS, tm), F32),
                        pltpu.VMEM((d, tm), F32)],
        compiler_params=_cparams(("arbitrary", "arbitrary")),
        name="peer",
    )(h2t, wq_t, k1, k2, down, up_t, x1, g2, norm_f)


def _rope_tables(rows, cols):
    r, c = jnp.meshgrid(jnp.arange(rows, dtype=F32), jnp.arange(cols, dtype=F32), indexing="ij")
    n_freq = RET_DK // 4
    inv = ROPE_BASE ** (-jnp.arange(n_freq, dtype=F32) / n_freq)
    ang = jnp.concatenate([r.reshape(-1, 1) * inv, c.reshape(-1, 1) * inv], axis=-1)
    cos, sin = jnp.cos(ang), jnp.sin(ang)
    return jnp.concatenate([cos, cos], axis=-1), jnp.concatenate([-sin, sin], axis=-1)


def kernel(x, c, ctx, c_ctx, w_mod, b_mod, norm1_w, norm2_w, w_in, ret_decay_f, ret_decay_b,
           gla_gk_up_f, gla_gk_bias_f, gla_gk_up_b, gla_gk_bias_b, w_out, peer_w_q, peer_k1, peer_k2,
           peer_down, peer_up, norm_f_w):
    depth = w_mod.shape[0]
    assert depth == 1
    b, l, d = x.shape
    lc = ctx.shape[1]
    li = 0

    cc = jnp.zeros((8, d), F32).at[:b].set(c).at[b].set(c_ctx)
    mod = _modulation(cc, w_mod[li], b_mod[li][None, :])
    sh1, sc1, g1, sh2, sc2, g2 = [mod[:b, j * d:(j + 1) * d][:, None, :] for j in range(N_MOD)]
    csh1, csc1 = [jnp.broadcast_to(mod[b:b + 1, j * d:(j + 1) * d][:, None, :], (b, 1, d)) for j in range(2)]

    w_main = w_in[li][:, :MAIN_COLS].astype(BF16)
    w_lr = jnp.zeros((d, LANES), F32).at[:, :2 * GLA_RANK].set(w_in[li][:, MAIN_COLS:])
    gk_up = jnp.zeros((LANES, 2 * GLA_KW), F32)
    gk_up = gk_up.at[:GLA_RANK, :GLA_KW].set(gla_gk_up_f[li])
    gk_up = gk_up.at[GLA_RANK:2 * GLA_RANK, GLA_KW:].set(gla_gk_up_b[li])
    gk_bias = jnp.concatenate([gla_gk_bias_f[li], gla_gk_bias_b[li]])[None, :]
    n1 = norm1_w[li][None, :]
    dec_f = jnp.zeros((8, LANES), F32).at[:RET_HEADS].set(jnp.broadcast_to(ret_decay_f[li][:, None], (RET_HEADS, LANES)))
    dec_b = jnp.zeros((8, LANES), F32).at[:RET_HEADS].set(jnp.broadcast_to(ret_decay_b[li][:, None], (RET_HEADS, LANES)))
    cos, sin = _rope_tables(l // GRID_W, GRID_W)
    cos_c = jnp.ones((lc, RET_DK), F32)
    sin_c = jnp.zeros((lc, RET_DK), F32)

    fc = _inproj(ctx, csh1, csc1, n1, w_main, w_lr, gk_up, gk_bias, cos_c, sin_c, tm=256)
    zr = jnp.zeros((b, RET_HEADS, RET_DK, RET_DK), F32)
    zg = jnp.zeros((b, GLA_HEADS, GLA_DK, GLA_DV), F32)
    ctx_out = _scan((fc[0], fc[1], fc[2], fc[4], fc[5], fc[6], fc[8], fc[9]), dec_f, dec_b,
                    (zr, zr, zg, zg), tb=256)
    ctx_states = ctx_out[4:]

    fl = _inproj(x, sh1, sc1, n1, w_main, w_lr, gk_up, gk_bias, cos, sin, tm=256)
    lat = _scan((fl[0], fl[1], fl[2], fl[4], fl[5], fl[6], fl[8], fl[9]), dec_f, dec_b,
                ctx_states, tb=256)

    x1, h2t = _mix(lat[0], lat[1], lat[2], lat[3], fl[3], fl[7], x, g1, sh2, sc2,
                   norm2_w[li][None, :], w_out[li].astype(BF16), tm=256)

    wq_t = peer_w_q[li].T.astype(BF16)
    return _peer(h2t, wq_t, peer_k1[li].astype(BF16), peer_k2[li].astype(BF16),
                 peer_down[li].astype(BF16), peer_up[li].T.astype(BF16), x1, g2, norm_f_w[None, :],
                 tm=512, ib=16)
```

```python
import functools

import jax
import jax.numpy as jnp
from jax import lax
from jax.experimental import pallas as pl
from jax.experimental.pallas import tpu as pltpu

F32 = jnp.float32
BF16 = jnp.bfloat16
HIGHEST = lax.Precision.HIGHEST

D_MODEL = 1024
GRID_W = 64
RET_HEADS = 4
RET_WIDTH = 512
RET_DK = 128
GLA_HEADS = 4
GLA_WIDTH = 512
GLA_DV = 128
GLA_DK = 64
GLA_KW = GLA_HEADS * GLA_DK
GLA_RANK = 16
GLA_GATE_NORM = 16.0
CHUNK = 64
ROPE_BASE = 10000.0
N_KEYS = 128
PEER_HEADS = 8
PEER_TOPK = 16
PEER_DQ = 256
N_MOD = 6
EPS = 1e-6
MAIN_COLS = 3584
LANES = 128
SUBLANES = 8
PACK16 = 16
MXU_DEPTH = 256
VMEM_LIMIT = 56 * 1024 * 1024


def _cparams(sem):
    return pltpu.CompilerParams(dimension_semantics=sem, vmem_limit_bytes=VMEM_LIMIT)


def _silu(v):
    return v / (1.0 + jnp.exp(-v))


def _log_sigmoid(z):
    return jnp.minimum(z, 0.0) - jnp.log(1.0 + jnp.exp(-jnp.abs(z)))


def _dot(a, b):
    return jnp.dot(a, b, preferred_element_type=F32)


def _dot_nt(a, b):
    return lax.dot_general(a, b, (((1,), (1,)), ((), ())), preferred_element_type=F32)


def _mod_kernel(c_ref, w_ref, b_ref, o_ref):
    a = _silu(c_ref[...])
    o_ref[...] = jnp.dot(a, w_ref[...], precision=HIGHEST, preferred_element_type=F32) + b_ref[...]


def _modulation(cc, w_mod, b_mod):
    n = w_mod.shape[1]
    tn = 1536
    return pl.pallas_call(
        _mod_kernel,
        grid=(n // tn,),
        in_specs=[pl.BlockSpec((8, D_MODEL), lambda j: (0, 0)),
                  pl.BlockSpec((D_MODEL, tn), lambda j: (0, j)),
                  pl.BlockSpec((1, tn), lambda j: (0, j))],
        out_specs=pl.BlockSpec((8, tn), lambda j: (0, j)),
        out_shape=jax.ShapeDtypeStruct((8, n), F32),
        compiler_params=_cparams(("arbitrary",)),
        name="modulation",
    )(cc, w_mod, b_mod)


def _inproj_kernel(x_ref, sh_ref, sc_ref, nw_ref, wm_ref, wl_ref, up_ref, gb_ref, cos_ref, sin_ref,
                   rq_ref, rk_ref, rv_ref, rg_ref, gq_ref, gk_ref, gv_ref, gg_ref, laf_ref, lab_ref):
    x = x_ref[0]
    ms = jnp.mean(x * x, axis=-1, keepdims=True)
    h = x * lax.rsqrt(ms + EPS) * nw_ref[...]
    h = h * (1.0 + sc_ref[0]) + sh_ref[0]
    p = _dot(h.astype(BF16), wm_ref[...])
    cos = cos_ref[...]
    sin = sin_ref[...]
    k_scale = RET_DK ** -0.5
    for hh in range(RET_HEADS):
        lo = hh * RET_DK
        q = p[:, lo:lo + RET_DK]
        k = p[:, RET_WIDTH + lo:RET_WIDTH + lo + RET_DK] * k_scale
        rq_ref[0, :, lo:lo + RET_DK] = q * cos + pltpu.roll(q, RET_DK // 2, axis=1) * sin
        rk_ref[0, :, lo:lo + RET_DK] = k * cos + pltpu.roll(k, RET_DK // 2, axis=1) * sin
    rv_ref[0] = p[:, 1024:1536]
    rg_ref[0] = p[:, 1536:2048]
    gq_ref[0] = p[:, 2048:2304] * (GLA_DK ** -0.5)
    gk_ref[0] = p[:, 2304:2560]
    gv_ref[0] = p[:, 2560:3072]
    gg_ref[0] = p[:, 3072:3584]
    lr = jnp.dot(h, wl_ref[...], precision=HIGHEST, preferred_element_type=F32)
    pre = jnp.dot(lr, up_ref[...], precision=HIGHEST, preferred_element_type=F32) + gb_ref[...]
    la = _log_sigmoid(pre) * (1.0 / GLA_GATE_NORM)
    laf_ref[0] = la[:, :GLA_KW]
    lab_ref[0] = la[:, GLA_KW:]


def _inproj(x, shift, scale, norm_w, w_main, w_lr, gk_up, gk_bias, cos, sin, tm):
    b, l, d = x.shape
    nt = l // tm
    tok = lambda w: pl.BlockSpec((1, tm, w), lambda bi, i: (bi, i, 0))
    full = lambda shp: pl.BlockSpec(shp, lambda bi, i: tuple(0 for _ in shp))
    vec = pl.BlockSpec((1, 1, d), lambda bi, i: (bi, 0, 0))
    widths = (RET_WIDTH, RET_WIDTH, RET_WIDTH, RET_WIDTH, GLA_KW, GLA_KW, GLA_WIDTH, GLA_WIDTH, GLA_KW, GLA_KW)
    return pl.pallas_call(
        _inproj_kernel,
        grid=(b, nt),
        in_specs=[tok(d), vec, vec, full((1, d)), full((d, MAIN_COLS)), full((d, LANES)),
                  full((LANES, 2 * GLA_KW)), full((1, 2 * GLA_KW)),
                  pl.BlockSpec((tm, RET_DK), lambda bi, i: (i, 0)),
                  pl.BlockSpec((tm, RET_DK), lambda bi, i: (i, 0))],
        out_specs=[tok(w) for w in widths],
        out_shape=[jax.ShapeDtypeStruct((b, l, w), F32) for w in widths],
        compiler_params=_cparams(("arbitrary", "arbitrary")),
        name="inproj",
    )(x, shift, scale, norm_w, w_main, w_lr, gk_up, gk_bias, cos, sin)


def _scan_kernel(decf_ref, decb_ref,
                 rqf_ref, rkf_ref, rvf_ref, rqb_ref, rkb_ref, rvb_ref,
                 gqf_ref, gkf_ref, gvf_ref, laf_ref, gqb_ref, gkb_ref, gvb_ref, lab_ref,
                 s0rf_ref, s0rb_ref, s0gf_ref, s0gb_ref,
                 orf_ref, orb_ref, ogf_ref, ogb_ref,
                 srf_out, srb_out, sgf_out, sgb_out,
                 srf, srb, sgf, sgb, *, nch):
    i = pl.program_id(1)

    @pl.when(i == 0)
    def _():
        srf[...] = s0rf_ref[0]
        srb[...] = s0rb_ref[0]
        sgf[...] = s0gf_ref[0]
        sgb[...] = s0gb_ref[0]

    C = CHUNK
    row = lax.broadcasted_iota(jnp.int32, (C, C), 0)
    col = lax.broadcasted_iota(jnp.int32, (C, C), 1)
    rel_f = (row - col).astype(F32)
    rel_b = (col - row).astype(F32)
    causal = row >= col
    anti = col >= row
    tri_lo = jnp.where(causal, 1.0, 0.0).astype(F32)
    tri_up = jnp.where(anti, 1.0, 0.0).astype(F32)
    pos_col = lax.broadcasted_iota(jnp.int32, (C, LANES), 0).astype(F32)

    lgf = _log_sigmoid(decf_ref[...])
    lgb = _log_sigmoid(decb_ref[...])

    def ret_consts(lg, h, fwd):
        lrow = lg[h:h + 1, :]
        rel = rel_f if fwd else rel_b
        mask = jnp.where(rel >= 0, jnp.exp(lrow[:, :C] * jnp.maximum(rel, 0.0)), 0.0)
        if fwd:
            qd = jnp.exp(lrow * (pos_col + 1.0))
            kd = jnp.exp(lrow * (C - 1.0 - pos_col))
        else:
            qd = jnp.exp(lrow * (C - pos_col))
            kd = jnp.exp(lrow * pos_col)
        cd = jnp.exp(lrow * float(C))
        return mask, qd, kd, cd

    cf = [ret_consts(lgf, h, True) for h in range(RET_HEADS)]
    cb = [ret_consts(lgb, h, False) for h in range(RET_HEADS)]

    def ret_chunk(q_ref, k_ref, v_ref, o_ref, s_ref, consts, r0):
        for h in range(RET_HEADS):
            mask, qd, kd, cd = consts[h]
            lo = h * RET_DK
            q = q_ref[0, r0:r0 + C, lo:lo + RET_DK]
            k = k_ref[0, r0:r0 + C, lo:lo + RET_DK]
            v = v_ref[0, r0:r0 + C, lo:lo + RET_DK].astype(BF16)
            s = s_ref[h]
            scores = _dot_nt(q.astype(BF16), k.astype(BF16)) * mask
            o = _dot(scores.astype(BF16), v) + _dot((q * qd).astype(BF16), s.astype(BF16))
            o_ref[0, r0:r0 + C, lo:lo + RET_DK] = o
            s_ref[h] = cd * s + _dot((k * kd).T.astype(BF16), v)

    def gla_chunk(q_ref, k_ref, v_ref, a_ref, o_ref, s_ref, fwd, r0):
        a = a_ref[0, r0:r0 + C, :]
        tri = tri_lo if fwd else tri_up
        g = jnp.dot(tri, a, precision=HIGHEST, preferred_element_type=F32)
        g_last = g[C - 1:C, :] if fwd else g[0:1, :]
        eg = jnp.exp(g)
        qt = q_ref[0, r0:r0 + C, :] * eg
        kk = k_ref[0, r0:r0 + C, :]
        kt = kk * jnp.exp(-g)
        kl = kk * jnp.exp(g_last - g)
        tot_col = jnp.exp(jnp.sum(a.T, axis=1, keepdims=True))
        keep = causal if fwd else anti
        for h in range(GLA_HEADS):
            lo = h * GLA_DK
            vo = h * GLA_DV
            v = v_ref[0, r0:r0 + C, vo:vo + GLA_DV].astype(BF16)
            s = s_ref[h]
            qh = qt[:, lo:lo + GLA_DK].astype(BF16)
            scores = jnp.where(keep, _dot_nt(qh, kt[:, lo:lo + GLA_DK].astype(BF16)), 0.0)
            o = _dot(scores.astype(BF16), v) + _dot(qh, s.astype(BF16))
            o_ref[0, r0:r0 + C, vo:vo + GLA_DV] = o
            s_ref[h] = tot_col[lo:lo + GLA_DK, :] * s + _dot(kl[:, lo:lo + GLA_DK].T.astype(BF16), v)

    for c in range(nch):
        rf = c * C
        rb = (nch - 1 - c) * C
        ret_chunk(rqf_ref, rkf_ref, rvf_ref, orf_ref, srf, cf, rf)
        ret_chunk(rqb_ref, rkb_ref, rvb_ref, orb_ref, srb, cb, rb)
        gla_chunk(gqf_ref, gkf_ref, gvf_ref, laf_ref, ogf_ref, sgf, True, rf)
        gla_chunk(gqb_ref, gkb_ref, gvb_ref, lab_ref, ogb_ref, sgb, False, rb)

    @pl.when(i == pl.num_programs(1) - 1)
    def _():
        srf_out[0] = srf[...]
        srb_out[0] = srb[...]
        sgf_out[0] = sgf[...]
        sgb_out[0] = sgb[...]


def _scan(feats, dec_f, dec_b, states, tb):
    rq, rk, rv, gq, gk, gv, la_f, la_b = feats
    b, l, _ = rq.shape
    nb = l // tb
    fwd = lambda w: pl.BlockSpec((1, tb, w), lambda bi, i: (bi, i, 0))
    bwd = lambda w: pl.BlockSpec((1, tb, w), lambda bi, i: (bi, nb - 1 - i, 0))
    dec = pl.BlockSpec((8, LANES), lambda bi, i: (0, 0))
    st_r = pl.BlockSpec((1, RET_HEADS, RET_DK, RET_DK), lambda bi, i: (bi, 0, 0, 0))
    st_g = pl.BlockSpec((1, GLA_HEADS, GLA_DK, GLA_DV), lambda bi, i: (bi, 0, 0, 0))
    o_shape = jax.ShapeDtypeStruct((b, l, RET_WIDTH), F32)
    sr_shape = jax.ShapeDtypeStruct((b, RET_HEADS, RET_DK, RET_DK), F32)
    sg_shape = jax.ShapeDtypeStruct((b, GLA_HEADS, GLA_DK, GLA_DV), F32)
    return pl.pallas_call(
        functools.partial(_scan_kernel, nch=tb // CHUNK),
        grid=(b, nb),
        in_specs=[dec, dec,
                  fwd(RET_WIDTH), fwd(RET_WIDTH), fwd(RET_WIDTH),
                  bwd(RET_WIDTH), bwd(RET_WIDTH), bwd(RET_WIDTH),
                  fwd(GLA_KW), fwd(GLA_KW), fwd(GLA_WIDTH), fwd(GLA_KW),
                  bwd(GLA_KW), bwd(GLA_KW), bwd(GLA_WIDTH), bwd(GLA_KW),
                  st_r, st_r, st_g, st_g],
        out_specs=[fwd(RET_WIDTH), bwd(RET_WIDTH), fwd(GLA_WIDTH), bwd(GLA_WIDTH),
                   st_r, st_r, st_g, st_g],
        out_shape=[o_shape, o_shape, o_shape, o_shape, sr_shape, sr_shape, sg_shape, sg_shape],
        scratch_shapes=[pltpu.VMEM((RET_HEADS, RET_DK, RET_DK), F32),
                        pltpu.VMEM((RET_HEADS, RET_DK, RET_DK), F32),
                        pltpu.VMEM((GLA_HEADS, GLA_DK, GLA_DV), F32),
                        pltpu.VMEM((GLA_HEADS, GLA_DK, GLA_DV), F32)],
        compiler_params=_cparams(("arbitrary", "arbitrary")),
        name="bidir_scan",
    )(dec_f, dec_b, rq, rk, rv, rq, rk, rv, gq, gk, gv, la_f, gq, gk, gv, la_b, *states)


def _mix_kernel(orf_ref, orb_ref, ogf_ref, ogb_ref, rg_ref, gg_ref, x_ref, g1_ref, sh_ref, sc_ref,
                nw_ref, wo_ref, x1_ref, h2t_ref):
    def head_norm(o):
        parts = []
        for h in range(RET_HEADS):
            oh = o[:, h * 128:(h + 1) * 128]
            parts.append(oh * lax.rsqrt(jnp.mean(oh * oh, axis=-1, keepdims=True) + EPS))
        return jnp.concatenate(parts, axis=-1)

    ret = head_norm(orf_ref[0] + orb_ref[0]) * _silu(rg_ref[0])
    gla = head_norm(ogf_ref[0] + ogb_ref[0]) * _silu(gg_ref[0])
    y = jnp.concatenate([ret, gla], axis=-1).astype(BF16)
    x1 = x_ref[0] + g1_ref[0] * _dot(y, wo_ref[...])
    x1_ref[0] = x1
    ms = jnp.mean(x1 * x1, axis=-1, keepdims=True)
    h2 = x1 * lax.rsqrt(ms + EPS) * nw_ref[...]
    h2 = h2 * (1.0 + sc_ref[0]) + sh_ref[0]
    h2t_ref[...] = h2.T.astype(BF16)


def _mix(o_rf, o_rb, o_gf, o_gb, rg, gg, x, g1, sh2, sc2, norm_w, w_out, tm):
    b, l, d = x.shape
    nt = l // tm
    tok = lambda w: pl.BlockSpec((1, tm, w), lambda bi, i: (bi, i, 0))
    vec = pl.BlockSpec((1, 1, d), lambda bi, i: (bi, 0, 0))
    return pl.pallas_call(
        _mix_kernel,
        grid=(b, nt),
        in_specs=[tok(512)] * 6 + [tok(d), vec, vec, vec,
                                   pl.BlockSpec((1, d), lambda bi, i: (0, 0)),
                                   pl.BlockSpec((d, d), lambda bi, i: (0, 0))],
        out_specs=[tok(d), pl.BlockSpec((d, tm), lambda bi, i: (0, bi * nt + i))],
        out_shape=[jax.ShapeDtypeStruct((b, l, d), F32), jax.ShapeDtypeStruct((d, b * l), BF16)],
        compiler_params=_cparams(("arbitrary", "arbitrary")),
        name="mix_outproj",
    )(o_rf, o_rb, o_gf, o_gb, rg, gg, x, g1, sh2, sc2, norm_w, w_out)


def _oddeven_merge_sort_pairs(n):
    pairs = []
    p = 1
    while p < n:
        k = p
        while k >= 1:
            for j in range(k % p, n - k, 2 * k):
                for i in range(min(k, n - j - k)):
                    if (i + j) // (2 * p) == (i + j + k) // (2 * p):
                        pairs.append((i + j, i + j + k))
            k //= 2
        p *= 2
    return pairs


def _bitonic_merge_pairs(n):
    pairs = []
    d = n // 2
    while d >= 1:
        pairs.extend((i, i + d) for i in range(n) if (i & d) == 0)
        d //= 2
    return pairs


_SORT16 = _oddeven_merge_sort_pairs(PEER_TOPK)
_MERGE16 = _bitonic_merge_pairs(PEER_TOPK)


def _compare_exchange(x, pairs):
    for a, b in pairs:
        hi = jnp.maximum(x[a], x[b])
        lo = jnp.minimum(x[a], x[b])
        x[a] = hi
        x[b] = lo


def _sublane_allreduce(v, op):
    for sh in (4, 2, 1):
        v = op(v, pltpu.roll(v, sh, axis=0))
    return v


def _top16_desc(s):
    x = [s[SUBLANES * k:SUBLANES * (k + 1), :] for k in range(PEER_TOPK)]
    _compare_exchange(x, _SORT16)
    for sh in (4, 2, 1):
        c = [jnp.maximum(x[k], pltpu.roll(x[PEER_TOPK - 1 - k], sh, axis=0)) for k in range(PEER_TOPK)]
        _compare_exchange(c, _MERGE16)
        x = c
    return x


def _route_group(s1, s2):
    inf = jnp.inf
    v1 = _top16_desc(s1)
    v2 = _top16_desc(s2)
    sub = lax.broadcasted_iota(jnp.int32, v1[0].shape, 0)

    def pack_rows(v):
        out = v[0]
        for r in range(1, SUBLANES):
            out = jnp.where(sub == r, v[r], out)
        return out

    v1lo, v1hi = pack_rows(v1[:SUBLANES]), pack_rows(v1[SUBLANES:])
    v2lo, v2hi = pack_rows(v2[:SUBLANES]), pack_rows(v2[SUBLANES:])
    cands = [v1[0] + v2lo, v1[0] + v2hi] + [v1[a] + v2lo for a in range(1, SUBLANES)] + [v1hi + v2[0]]
    c = list(cands)
    tau = None
    for _ in range(PEER_TOPK):
        m = c[0]
        for ci in c[1:]:
            m = jnp.maximum(m, ci)
        tau = _sublane_allreduce(m, jnp.maximum)
        c = [jnp.where(ci == tau, -inf, ci) for ci in c]
    m12 = v1[0] + v2[0]
    zs = None
    for cd in cands:
        t = jnp.where(cd >= tau, jnp.exp(cd - m12), 0.0)
        zs = t if zs is None else zs + t
    zinv = 1.0 / _sublane_allreduce(zs, jnp.add)
    thetas = []
    for b in range(PEER_TOPK):
        lo = jnp.where(v1lo + v2[b] >= tau, v1lo, inf)
        hi = jnp.where(v1hi + v2[b] >= tau, v1hi, inf)
        thetas.append(_sublane_allreduce(jnp.minimum(lo, hi), jnp.minimum))
    cnt, e1, rank, e2 = [], [], [], []
    for k in range(N_KEYS // SUBLANES):
        x = s1[SUBLANES * k:SUBLANES * (k + 1), :]
        y = s2[SUBLANES * k:SUBLANES * (k + 1), :]
        ck = jnp.where(x >= thetas[0], 1.0, 0.0)
        rk = jnp.where(v2[0] > y, 1.0, 0.0)
        for b in range(1, PEER_TOPK):
            ck = ck + jnp.where(x >= thetas[b], 1.0, 0.0)
            rk = rk + jnp.where(v2[b] > y, 1.0, 0.0)
        cnt.append(ck)
        rank.append(rk)
        e1.append(jnp.exp(x - v1[0]) * zinv)
        e2.append(jnp.exp(y - v2[0]))
    return cnt, e1, rank, e2


def _peer_kernel(h2t_ref, wqt_ref, k1_ref, k2_ref, down_ref, upt_ref, x1_ref, g2_ref, nf_ref,
                 out_ref,
                 rank2_s, e2_s, cnt_s, e1_s, acc_s, *, ib):
    e = pl.program_id(1)
    tm = h2t_ref.shape[1]
    half = PEER_DQ // 2
    nk2 = N_KEYS // PACK16

    @pl.when(e == 0)
    def _():
        acc_s[...] = jnp.zeros_like(acc_s)

        def head_body(h, carry):
            ht = h2t_ref[...]
            row0 = pl.multiple_of(h * PEER_DQ, PEER_DQ)
            q = _dot(wqt_ref[pl.ds(row0, PEER_DQ), :], ht).astype(BF16)
            s1 = _dot(k1_ref[h], q[:half, :])
            s2 = _dot(k2_ref[h], q[half:, :])
            for g in range(tm // LANES):
                sl = slice(g * LANES, (g + 1) * LANES)
                cnt, e1, rank, e2 = _route_group(s1[:, sl], s2[:, sl])
                cnt_s[h, g] = jnp.concatenate(cnt, axis=0)
                e1_s[h, g] = jnp.concatenate(e1, axis=0)
                for k2 in range(nk2):
                    rank2_s[h, k2, :, sl] = jnp.concatenate(rank[2 * k2:2 * k2 + 2], axis=0).astype(BF16)
                    e2_s[h, k2, :, sl] = jnp.concatenate(e2[2 * k2:2 * k2 + 2], axis=0).astype(BF16)
            return carry

        lax.fori_loop(0, PEER_HEADS, head_body, 0)

    a_t = _dot(down_ref[...], h2t_ref[...])
    hs = []
    for ii in range(ib):
        i = e * ib + ii
        w = jnp.zeros((nk2, PACK16, tm), BF16)
        for h in range(PEER_HEADS):
            c_row = jnp.concatenate([cnt_s[h, g, pl.ds(i, PACK16, stride=0), :] for g in range(tm // LANES)],
                                    axis=1).astype(BF16)
            e_row = jnp.concatenate([e1_s[h, g, pl.ds(i, PACK16, stride=0), :] for g in range(tm // LANES)],
                                    axis=1).astype(BF16)
            w = w + jnp.where(rank2_s[h] < c_row[None], e2_s[h], jnp.zeros((), BF16)) * e_row[None]
        a = a_t[ii * N_KEYS:(ii + 1) * N_KEYS, :]
        act = 0.5 * a * (1.0 + lax.erf(a * (2.0 ** -0.5)))
        hs.append((w * act.astype(BF16).reshape(nk2, PACK16, tm)).reshape(N_KEYS, tm))
    acc_s[...] += _dot(upt_ref[...], jnp.concatenate(hs, axis=0))

    @pl.when(e == pl.num_programs(1) - 1)
    def _():
        y = x1_ref[0] + g2_ref[0] * acc_s[...].T
        ms = jnp.mean(y * y, axis=-1, keepdims=True)
        out_ref[0] = y * lax.rsqrt(ms + EPS) * nf_ref[...]


def _peer(h2t, wq_t, k1, k2, down, up_t, x1, g2, norm_f, tm, ib):
    b, l, d = x1.shape
    nt = l // tm
    ne = N_KEYS // ib
    te = ib * N_KEYS
    const = lambda shp: pl.BlockSpec(shp, lambda t, e: tuple(0 for _ in shp))
    return pl.pallas_call(
        functools.partial(_peer_kernel, ib=ib),
        grid=(b * nt, ne),
        in_specs=[pl.BlockSpec((d, tm), lambda t, e: (0, t)),
                  const((PEER_HEADS * PEER_DQ, d)),
                  const((PEER_HEADS, N_KEYS, PEER_DQ // 2)),
                  const((PEER_HEADS, N_KEYS, PEER_DQ // 2)),
                  pl.BlockSpec((te, d), lambda t, e: (e, 0)),
                  pl.BlockSpec((d, te), lambda t, e: (0, e)),
                  pl.BlockSpec((1, tm, d), lambda t, e: (t // nt, t % nt, 0)),
                  pl.BlockSpec((1, 1, d), lambda t, e: (t // nt, 0, 0)),
                  const((1, d))],
        out_specs=pl.BlockSpec((1, tm, d), lambda t, e: (t // nt, t % nt, 0)),
        out_shape=jax.ShapeDtypeStruct((b, l, d), F32),
        scratch_shapes=[pltpu.VMEM((PEER_HEADS, N_KEYS // PACK16, PACK16, tm), BF16),
                        pltpu.VMEM((PEER_HEADS, N_KEYS // PACK16, PACK16, tm), BF16),
                        pltpu.VMEM((PEER_HEADS, tm // LANES, N_KEYS, LANES), F32),
                        pltpu.VMEM((PEER_HEADS, tm // LANES, N_KEYS, LANES), F32),
                        pltpu.VMEM((d, tm), F32)],
        compiler_params=_cparams(("arbitrary", "arbitrary")),
        name="peer",
    )(h2t, wq_t, k1, k2, down, up_t, x1, g2, norm_f)


def _rope_tables(rows, cols):
    r, c = jnp.meshgrid(jnp.arange(rows, dtype=F32), jnp.arange(cols, dtype=F32), indexing="ij")
    n_freq = RET_DK // 4
    inv = ROPE_BASE ** (-jnp.arange(n_freq, dtype=F32) / n_freq)
    ang = jnp.concatenate([r.reshape(-1, 1) * inv, c.reshape(-1, 1) * inv], axis=-1)
    cos, sin = jnp.cos(ang), jnp.sin(ang)
    return jnp.concatenate([cos, cos], axis=-1), jnp.concatenate([-sin, sin], axis=-1)


def kernel(x, c, ctx, c_ctx, w_mod, b_mod, norm1_w, norm2_w, w_in, ret_decay_f, ret_decay_b,
           gla_gk_up_f, gla_gk_bias_f, gla_gk_up_b, gla_gk_bias_b, w_out, peer_w_q, peer_k1, peer_k2,
           peer_down, peer_up, norm_f_w):
    depth = w_mod.shape[0]
    assert depth == 1
    b, l, d = x.shape
    lc = ctx.shape[1]
    li = 0

    cc = jnp.zeros((8, d), F32).at[:b].set(c).at[b].set(c_ctx)
    mod = _modulation(cc, w_mod[li], b_mod[li][None, :])
    sh1, sc1, g1, sh2, sc2, g2 = [mod[:b, j * d:(j + 1) * d][:, None, :] for j in range(N_MOD)]
    csh1, csc1 = [jnp.broadcast_to(mod[b:b + 1, j * d:(j + 1) * d][:, None, :], (b, 1, d)) for j in range(2)]

    w_main = w_in[li][:, :MAIN_COLS].astype(BF16)
    w_lr = jnp.zeros((d, LANES), F32).at[:, :2 * GLA_RANK].set(w_in[li][:, MAIN_COLS:])
    gk_up = jnp.zeros((LANES, 2 * GLA_KW), F32)
    gk_up = gk_up.at[:GLA_RANK, :GLA_KW].set(gla_gk_up_f[li])
    gk_up = gk_up.at[GLA_RANK:2 * GLA_RANK, GLA_KW:].set(gla_gk_up_b[li])
    gk_bias = jnp.concatenate([gla_gk_bias_f[li], gla_gk_bias_b[li]])[None, :]
    n1 = norm1_w[li][None, :]
    dec_f = jnp.zeros((8, LANES), F32).at[:RET_HEADS].set(jnp.broadcast_to(ret_decay_f[li][:, None], (RET_HEADS, LANES)))
    dec_b = jnp.zeros((8, LANES), F32).at[:RET_HEADS].set(jnp.broadcast_to(ret_decay_b[li][:, None], (RET_HEADS, LANES)))
    cos, sin = _rope_tables(l // GRID_W, GRID_W)
    cos_c = jnp.ones((lc, RET_DK), F32)
    sin_c = jnp.zeros((lc, RET_DK), F32)

    fc = _inproj(ctx, csh1, csc1, n1, w_main, w_lr, gk_up, gk_bias, cos_c, sin_c, tm=256)
    zr = jnp.zeros((b, RET_HEADS, RET_DK, RET_DK), F32)
    zg = jnp.zeros((b, GLA_HEADS, GLA_DK, GLA_DV), F32)
    ctx_out = _scan((fc[0], fc[1], fc[2], fc[4], fc[5], fc[6], fc[8], fc[9]), dec_f, dec_b,
                    (zr, zr, zg, zg), tb=256)
    ctx_states = ctx_out[4:]

    fl = _inproj(x, sh1, sc1, n1, w_main, w_lr, gk_up, gk_bias, cos, sin, tm=256)
    lat = _scan((fl[0], fl[1], fl[2], fl[4], fl[5], fl[6], fl[8], fl[9]), dec_f, dec_b,
                ctx_states, tb=256)

    x1, h2t = _mix(lat[0], lat[1], lat[2], lat[3], fl[3], fl[7], x, g1, sh2, sc2,
                   norm2_w[li][None, :], w_out[li].astype(BF16), tm=256)

    wq_t = peer_w_q[li].T.astype(BF16)
    return _peer(h2t, wq_t, peer_k1[li].astype(BF16), peer_k2[li].astype(BF16),
                 peer_down[li].astype(BF16), peer_up[li].T.astype(BF16), x1, g2, norm_f_w[None, :],
                 tm=512, ib=16)
```

```python
import functools

import jax
import jax.numpy as jnp
from jax import lax
from jax.experimental import pallas as pl
from jax.experimental.pallas import tpu as pltpu

F32 = jnp.float32
BF16 = jnp.bfloat16
HIGHEST = lax.Precision.HIGHEST

D_MODEL = 1024
GRID_W = 64
RET_HEADS = 4
RET_WIDTH = 512
RET_DK = 128
GLA_HEADS = 4
GLA_WIDTH = 512
GLA_DV = 128
GLA_DK = 64
GLA_KW = GLA_HEADS * GLA_DK
GLA_RANK = 16
GLA_GATE_NORM = 16.0
CHUNK = 64
ROPE_BASE = 10000.0
N_KEYS = 128
PEER_HEADS = 8
PEER_TOPK = 16
PEER_DQ = 256
N_MOD = 6
EPS = 1e-6
MAIN_COLS = 3584
LANES = 128
SUBLANES = 8
PACK16 = 16
MXU_DEPTH = 256
VMEM_LIMIT = 56 * 1024 * 1024


def _cparams(sem):
    return pltpu.CompilerParams(dimension_semantics=sem, vmem_limit_bytes=VMEM_LIMIT)


def _silu(v):
    return v / (1.0 + jnp.exp(-v))


def _log_sigmoid(z):
    return jnp.minimum(z, 0.0) - jnp.log(1.0 + jnp.exp(-jnp.abs(z)))


def _dot(a, b):
    return jnp.dot(a, b, preferred_element_type=F32)


def _dot_nt(a, b):
    return lax.dot_general(a, b, (((1,), (1,)), ((), ())), preferred_element_type=F32)


def _mod_kernel(c_ref, w_ref, b_ref, o_ref):
    a = _silu(c_ref[...])
    o_ref[...] = jnp.dot(a, w_ref[...], precision=HIGHEST, preferred_element_type=F32) + b_ref[...]


def _modulation(cc, w_mod, b_mod):
    n = w_mod.shape[1]
    tn = 1536
    return pl.pallas_call(
        _mod_kernel,
        grid=(n // tn,),
        in_specs=[pl.BlockSpec((8, D_MODEL), lambda j: (0, 0)),
                  pl.BlockSpec((D_MODEL, tn), lambda j: (0, j)),
                  pl.BlockSpec((1, tn), lambda j: (0, j))],
        out_specs=pl.BlockSpec((8, tn), lambda j: (0, j)),
        out_shape=jax.ShapeDtypeStruct((8, n), F32),
        compiler_params=_cparams(("arbitrary",)),
        name="modulation",
    )(cc, w_mod, b_mod)


def _inproj_kernel(x_ref, sh_ref, sc_ref, nw_ref, wm_ref, up_ref, gb_ref, cos_ref, sin_ref,
                   rq_ref, rk_ref, rv_ref, rg_ref, gq_ref, gk_ref, gv_ref, gg_ref, laf_ref, lab_ref):
    x = x_ref[0]
    ms = jnp.mean(x * x, axis=-1, keepdims=True)
    h = x * lax.rsqrt(ms + EPS) * nw_ref[...]
    h = h * (1.0 + sc_ref[0]) + sh_ref[0]
    p = _dot(h.astype(BF16), wm_ref[...])
    cos = cos_ref[...]
    sin = sin_ref[...]
    k_scale = RET_DK ** -0.5
    for hh in range(RET_HEADS):
        lo = hh * RET_DK
        q = p[:, lo:lo + RET_DK]
        k = p[:, RET_WIDTH + lo:RET_WIDTH + lo + RET_DK] * k_scale
        rq_ref[0, :, lo:lo + RET_DK] = q * cos + pltpu.roll(q, RET_DK // 2, axis=1) * sin
        rk_ref[0, :, lo:lo + RET_DK] = k * cos + pltpu.roll(k, RET_DK // 2, axis=1) * sin
    rv_ref[0] = p[:, 1024:1536]
    rg_ref[0] = p[:, 1536:2048]
    gq_ref[0] = p[:, 2048:2304] * (GLA_DK ** -0.5)
    gk_ref[0] = p[:, 2304:2560]
    gv_ref[0] = p[:, 2560:3072]
    gg_ref[0] = p[:, 3072:3584]
    lr = p[:, MAIN_COLS:MAIN_COLS + LANES]
    pre = _dot(lr.astype(BF16), up_ref[...]) + gb_ref[...]
    la = _log_sigmoid(pre) * (1.0 / GLA_GATE_NORM)
    laf_ref[0] = la[:, :GLA_KW]
    lab_ref[0] = la[:, GLA_KW:]


def _inproj(x, shift, scale, norm_w, w_main, gk_up, gk_bias, cos, sin, tm):
    b, l, d = x.shape
    nt = l // tm
    tok = lambda w: pl.BlockSpec((1, tm, w), lambda bi, i: (bi, i, 0))
    full = lambda shp: pl.BlockSpec(shp, lambda bi, i: tuple(0 for _ in shp))
    vec = pl.BlockSpec((1, 1, d), lambda bi, i: (bi, 0, 0))
    widths = (RET_WIDTH, RET_WIDTH, RET_WIDTH, RET_WIDTH, GLA_KW, GLA_KW, GLA_WIDTH, GLA_WIDTH, GLA_KW, GLA_KW)
    return pl.pallas_call(
        _inproj_kernel,
        grid=(b, nt),
        in_specs=[tok(d), vec, vec, full((1, d)), full((d, MAIN_COLS + LANES)),
                  full((LANES, 2 * GLA_KW)), full((1, 2 * GLA_KW)),
                  pl.BlockSpec((tm, RET_DK), lambda bi, i: (i, 0)),
                  pl.BlockSpec((tm, RET_DK), lambda bi, i: (i, 0))],
        out_specs=[tok(w) for w in widths],
        out_shape=[jax.ShapeDtypeStruct((b, l, w), F32) for w in widths],
        compiler_params=_cparams(("arbitrary", "arbitrary")),
        name="inproj",
    )(x, shift, scale, norm_w, w_main, gk_up, gk_bias, cos, sin)


def _scan_kernel(decf_ref, decb_ref,
                 rqf_ref, rkf_ref, rvf_ref, rqb_ref, rkb_ref, rvb_ref,
                 gqf_ref, gkf_ref, gvf_ref, laf_ref, gqb_ref, gkb_ref, gvb_ref, lab_ref,
                 s0rf_ref, s0rb_ref, s0gf_ref, s0gb_ref,
                 orf_ref, orb_ref, ogf_ref, ogb_ref,
                 srf_out, srb_out, sgf_out, sgb_out,
                 srf, srb, sgf, sgb, *, nch):
    i = pl.program_id(1)

    @pl.when(i == 0)
    def _():
        srf[...] = s0rf_ref[0]
        srb[...] = s0rb_ref[0]
        sgf[...] = s0gf_ref[0]
        sgb[...] = s0gb_ref[0]

    C = CHUNK
    row = lax.broadcasted_iota(jnp.int32, (C, C), 0)
    col = lax.broadcasted_iota(jnp.int32, (C, C), 1)
    causal = row >= col
    anti = col >= row
    tri_lo = jnp.where(causal, 1.0, 0.0).astype(F32)
    tri_up = jnp.where(anti, 1.0, 0.0).astype(F32)

    lgf = _log_sigmoid(decf_ref[...])
    lgb = _log_sigmoid(decb_ref[...])

    CR = nch * C
    rrow = lax.broadcasted_iota(jnp.int32, (CR, CR), 0)
    rcol = lax.broadcasted_iota(jnp.int32, (CR, CR), 1)
    pos_r = lax.broadcasted_iota(jnp.int32, (CR, LANES), 0).astype(F32)

    def ret_consts(lg, h, fwd):
        lrow = lg[h:h + 1, :]
        lwide = jnp.concatenate([lrow] * (CR // LANES), axis=1)
        rel = ((rrow - rcol) if fwd else (rcol - rrow)).astype(F32)
        mask = jnp.where(rel >= 0, jnp.exp(lwide * jnp.maximum(rel, 0.0)), 0.0)
        if fwd:
            qd = jnp.exp(lrow * (pos_r + 1.0))
            kd = jnp.exp(lrow * (CR - 1.0 - pos_r))
        else:
            qd = jnp.exp(lrow * (CR - pos_r))
            kd = jnp.exp(lrow * pos_r)
        cd = jnp.exp(lrow * float(CR))
        return mask, qd, kd, cd

    def ret_block(q_ref, k_ref, v_ref, o_ref, s_ref, lg, fwd):
        for h in range(RET_HEADS):
            mask, qd, kd, cd = ret_consts(lg, h, fwd)
            lo = h * RET_DK
            q = q_ref[0, :, lo:lo + RET_DK]
            k = k_ref[0, :, lo:lo + RET_DK]
            v = v_ref[0, :, lo:lo + RET_DK].astype(BF16)
            s = s_ref[h]
            scores = _dot_nt(q.astype(BF16), k.astype(BF16)) * mask
            o = _dot(scores.astype(BF16), v) + _dot((q * qd).astype(BF16), s.astype(BF16))
            o_ref[0, :, lo:lo + RET_DK] = o
            s_ref[h] = cd * s + _dot((k * kd).T.astype(BF16), v)

    def gla_chunk(q_ref, k_ref, v_ref, a_ref, o_ref, s_ref, fwd, r0):
        a = a_ref[0, r0:r0 + C, :]
        tri = tri_lo if fwd else tri_up
        g = jnp.dot(tri, a, precision=HIGHEST, preferred_element_type=F32)
        g_last = g[C - 1:C, :] if fwd else g[0:1, :]
        eg = jnp.exp(g)
        qt = q_ref[0, r0:r0 + C, :] * eg
        kk = k_ref[0, r0:r0 + C, :]
        kt = kk * jnp.exp(-g)
        kl = kk * jnp.exp(g_last - g)
        tot_col = jnp.exp(jnp.sum(a.T, axis=1, keepdims=True))
        keep = causal if fwd else anti
        for h in range(GLA_HEADS):
            lo = h * GLA_DK
            vo = h * GLA_DV
            v = v_ref[0, r0:r0 + C, vo:vo + GLA_DV].astype(BF16)
            s = s_ref[h]
            qh = qt[:, lo:lo + GLA_DK].astype(BF16)
            scores = jnp.where(keep, _dot_nt(qh, kt[:, lo:lo + GLA_DK].astype(BF16)), 0.0)
            o = _dot(scores.astype(BF16), v) + _dot(qh, s.astype(BF16))
            o_ref[0, r0:r0 + C, vo:vo + GLA_DV] = o
            s_ref[h] = tot_col[lo:lo + GLA_DK, :] * s + _dot(kl[:, lo:lo + GLA_DK].T.astype(BF16), v)

    ret_block(rqf_ref, rkf_ref, rvf_ref, orf_ref, srf, lgf, True)
    ret_block(rqb_ref, rkb_ref, rvb_ref, orb_ref, srb, lgb, False)
    for c in range(nch):
        rf = c * C
        rb = (nch - 1 - c) * C
        gla_chunk(gqf_ref, gkf_ref, gvf_ref, laf_ref, ogf_ref, sgf, True, rf)
        gla_chunk(gqb_ref, gkb_ref, gvb_ref, lab_ref, ogb_ref, sgb, False, rb)

    @pl.when(i == pl.num_programs(1) - 1)
    def _():
        srf_out[0] = srf[...]
        srb_out[0] = srb[...]
        sgf_out[0] = sgf[...]
        sgb_out[0] = sgb[...]


def _scan(feats, dec_f, dec_b, states, tb):
    rq, rk, rv, gq, gk, gv, la_f, la_b = feats
    b, l, _ = rq.shape
    nb = l // tb
    fwd = lambda w: pl.BlockSpec((1, tb, w), lambda bi, i: (bi, i, 0))
    bwd = lambda w: pl.BlockSpec((1, tb, w), lambda bi, i: (bi, nb - 1 - i, 0))
    dec = pl.BlockSpec((8, LANES), lambda bi, i: (0, 0))
    st_r = pl.BlockSpec((1, RET_HEADS, RET_DK, RET_DK), lambda bi, i: (bi, 0, 0, 0))
    st_g = pl.BlockSpec((1, GLA_HEADS, GLA_DK, GLA_DV), lambda bi, i: (bi, 0, 0, 0))
    o_shape = jax.ShapeDtypeStruct((b, l, RET_WIDTH), F32)
    sr_shape = jax.ShapeDtypeStruct((b, RET_HEADS, RET_DK, RET_DK), F32)
    sg_shape = jax.ShapeDtypeStruct((b, GLA_HEADS, GLA_DK, GLA_DV), F32)
    return pl.pallas_call(
        functools.partial(_scan_kernel, nch=tb // CHUNK),
        grid=(b, nb),
        in_specs=[dec, dec,
                  fwd(RET_WIDTH), fwd(RET_WIDTH), fwd(RET_WIDTH),
                  bwd(RET_WIDTH), bwd(RET_WIDTH), bwd(RET_WIDTH),
                  fwd(GLA_KW), fwd(GLA_KW), fwd(GLA_WIDTH), fwd(GLA_KW),
                  bwd(GLA_KW), bwd(GLA_KW), bwd(GLA_WIDTH), bwd(GLA_KW),
                  st_r, st_r, st_g, st_g],
        out_specs=[fwd(RET_WIDTH), bwd(RET_WIDTH), fwd(GLA_WIDTH), bwd(GLA_WIDTH),
                   st_r, st_r, st_g, st_g],
        out_shape=[o_shape, o_shape, o_shape, o_shape, sr_shape, sr_shape, sg_shape, sg_shape],
        scratch_shapes=[pltpu.VMEM((RET_HEADS, RET_DK, RET_DK), F32),
                        pltpu.VMEM((RET_HEADS, RET_DK, RET_DK), F32),
                        pltpu.VMEM((GLA_HEADS, GLA_DK, GLA_DV), F32),
                        pltpu.VMEM((GLA_HEADS, GLA_DK, GLA_DV), F32)],
        compiler_params=_cparams(("arbitrary", "arbitrary")),
        name="bidir_scan",
    )(dec_f, dec_b, rq, rk, rv, rq, rk, rv, gq, gk, gv, la_f, gq, gk, gv, la_b, *states)


def _mix_kernel(orf_ref, orb_ref, ogf_ref, ogb_ref, rg_ref, gg_ref, x_ref, g1_ref, sh_ref, sc_ref,
                nw_ref, wo_ref, x1_ref, h2t_ref):
    def head_norm(o):
        parts = []
        for h in range(RET_HEADS):
            oh = o[:, h * 128:(h + 1) * 128]
            parts.append(oh * lax.rsqrt(jnp.mean(oh * oh, axis=-1, keepdims=True) + EPS))
        return jnp.concatenate(parts, axis=-1)

    ret = head_norm(orf_ref[0] + orb_ref[0]) * _silu(rg_ref[0])
    gla = head_norm(ogf_ref[0] + ogb_ref[0]) * _silu(gg_ref[0])
    y = jnp.concatenate([ret, gla], axis=-1).astype(BF16)
    x1 = x_ref[0] + g1_ref[0] * _dot(y, wo_ref[...])
    x1_ref[0] = x1
    ms = jnp.mean(x1 * x1, axis=-1, keepdims=True)
    h2 = x1 * lax.rsqrt(ms + EPS) * nw_ref[...]
    h2 = h2 * (1.0 + sc_ref[0]) + sh_ref[0]
    h2t_ref[...] = h2.T.astype(BF16)


def _mix(o_rf, o_rb, o_gf, o_gb, rg, gg, x, g1, sh2, sc2, norm_w, w_out, tm):
    b, l, d = x.shape
    nt = l // tm
    tok = lambda w: pl.BlockSpec((1, tm, w), lambda bi, i: (bi, i, 0))
    vec = pl.BlockSpec((1, 1, d), lambda bi, i: (bi, 0, 0))
    return pl.pallas_call(
        _mix_kernel,
        grid=(b, nt),
        in_specs=[tok(512)] * 6 + [tok(d), vec, vec, vec,
                                   pl.BlockSpec((1, d), lambda bi, i: (0, 0)),
                                   pl.BlockSpec((d, d), lambda bi, i: (0, 0))],
        out_specs=[tok(d), pl.BlockSpec((d, tm), lambda bi, i: (0, bi * nt + i))],
        out_shape=[jax.ShapeDtypeStruct((b, l, d), F32), jax.ShapeDtypeStruct((d, b * l), BF16)],
        compiler_params=_cparams(("arbitrary", "arbitrary")),
        name="mix_outproj",
    )(o_rf, o_rb, o_gf, o_gb, rg, gg, x, g1, sh2, sc2, norm_w, w_out)


def _oddeven_merge_sort_pairs(n):
    pairs = []
    p = 1
    while p < n:
        k = p
        while k >= 1:
            for j in range(k % p, n - k, 2 * k):
                for i in range(min(k, n - j - k)):
                    if (i + j) // (2 * p) == (i + j + k) // (2 * p):
                        pairs.append((i + j, i + j + k))
            k //= 2
        p *= 2
    return pairs


def _bitonic_merge_pairs(n):
    pairs = []
    d = n // 2
    while d >= 1:
        pairs.extend((i, i + d) for i in range(n) if (i & d) == 0)
        d //= 2
    return pairs


_SORT16 = _oddeven_merge_sort_pairs(PEER_TOPK)
_MERGE16 = _bitonic_merge_pairs(PEER_TOPK)


def _compare_exchange(x, pairs):
    for a, b in pairs:
        hi = jnp.maximum(x[a], x[b])
        lo = jnp.minimum(x[a], x[b])
        x[a] = hi
        x[b] = lo


def _sublane_allreduce(v, op):
    for sh in (4, 2, 1):
        v = op(v, pltpu.roll(v, sh, axis=0))
    return v


def _top16_desc(s):
    x = [s[SUBLANES * k:SUBLANES * (k + 1), :] for k in range(PEER_TOPK)]
    _compare_exchange(x, _SORT16)
    for sh in (4, 2, 1):
        c = [jnp.maximum(x[k], pltpu.roll(x[PEER_TOPK - 1 - k], sh, axis=0)) for k in range(PEER_TOPK)]
        _compare_exchange(c, _MERGE16)
        x = c
    return x


def _route_group(s1, s2):
    inf = jnp.inf
    v1 = _top16_desc(s1)
    v2 = _top16_desc(s2)
    sub = lax.broadcasted_iota(jnp.int32, v1[0].shape, 0)

    def pack_rows(v):
        out = v[0]
        for r in range(1, SUBLANES):
            out = jnp.where(sub == r, v[r], out)
        return out

    v1lo, v1hi = pack_rows(v1[:SUBLANES]), pack_rows(v1[SUBLANES:])
    v2lo, v2hi = pack_rows(v2[:SUBLANES]), pack_rows(v2[SUBLANES:])
    cands = [v1[0] + v2lo, v1[0] + v2hi] + [v1[a] + v2lo for a in range(1, SUBLANES)] + [v1hi + v2[0]]
    c = list(cands)
    tau = None
    for _ in range(PEER_TOPK):
        m = c[0]
        for ci in c[1:]:
            m = jnp.maximum(m, ci)
        tau = _sublane_allreduce(m, jnp.maximum)
        c = [jnp.where(ci == tau, -inf, ci) for ci in c]
    m12 = v1[0] + v2[0]
    zs = None
    for cd in cands:
        t = jnp.where(cd >= tau, jnp.exp(cd - m12), 0.0)
        zs = t if zs is None else zs + t
    zinv = 1.0 / _sublane_allreduce(zs, jnp.add)
    thetas = []
    for b in range(PEER_TOPK):
        lo = jnp.where(v1lo + v2[b] >= tau, v1lo, inf)
        hi = jnp.where(v1hi + v2[b] >= tau, v1hi, inf)
        thetas.append(_sublane_allreduce(jnp.minimum(lo, hi), jnp.minimum))
    cnt, e1, rank, e2 = [], [], [], []
    for k in range(N_KEYS // SUBLANES):
        x = s1[SUBLANES * k:SUBLANES * (k + 1), :]
        y = s2[SUBLANES * k:SUBLANES * (k + 1), :]
        ck = jnp.where(x >= thetas[0], 1.0, 0.0)
        rk = jnp.where(v2[0] > y, 1.0, 0.0)
        for b in range(1, PEER_TOPK):
            ck = ck + jnp.where(x >= thetas[b], 1.0, 0.0)
            rk = rk + jnp.where(v2[b] > y, 1.0, 0.0)
        cnt.append(ck)
        rank.append(rk)
        e1.append(jnp.exp(x - v1[0]) * zinv)
        e2.append(jnp.exp(y - v2[0]))
    return cnt, e1, rank, e2


def _peer_kernel(h2t_ref, wqt_ref, k1_ref, k2_ref, down_ref, upt_ref, x1_ref, g2_ref, nf_ref,
                 out_ref,
                 rank2_s, e2_s, cnt_s, e1_s, acc_s, *, ib):
    e = pl.program_id(1)
    tm = h2t_ref.shape[1]
    half = PEER_DQ // 2
    nk2 = N_KEYS // PACK16

    @pl.when(e == 0)
    def _():
        acc_s[...] = jnp.zeros_like(acc_s)

        def head_body(h, carry):
            ht = h2t_ref[...]
            row0 = pl.multiple_of(h * PEER_DQ, PEER_DQ)
            q = _dot(wqt_ref[pl.ds(row0, PEER_DQ), :], ht).astype(BF16)
            s1 = _dot(k1_ref[h], q[:half, :])
            s2 = _dot(k2_ref[h], q[half:, :])
            for g in range(tm // LANES):
                sl = slice(g * LANES, (g + 1) * LANES)
                cnt, e1, rank, e2 = _route_group(s1[:, sl], s2[:, sl])
                cnt_s[h, :, sl] = jnp.concatenate(cnt, axis=0)
                e1_s[h, :, sl] = jnp.concatenate(e1, axis=0)
                for k2 in range(nk2):
                    rank2_s[h, k2, :, sl] = jnp.concatenate(rank[2 * k2:2 * k2 + 2], axis=0).astype(BF16)
                    e2_s[h, k2, :, sl] = jnp.concatenate(e2[2 * k2:2 * k2 + 2], axis=0).astype(BF16)
            return carry

        lax.fori_loop(0, PEER_HEADS, head_body, 0)

    a_t = _dot(down_ref[...], h2t_ref[...])
    hs = []
    for ii in range(ib):
        i = e * ib + ii
        w = jnp.zeros((nk2, PACK16, tm), BF16)
        for h in range(PEER_HEADS):
            c_row = jnp.broadcast_to(cnt_s[h, pl.ds(i, 1), :], (PACK16, tm)).astype(BF16)
            e_row = jnp.broadcast_to(e1_s[h, pl.ds(i, 1), :], (PACK16, tm)).astype(BF16)
            w = w + jnp.where(rank2_s[h] < c_row[None], e2_s[h], jnp.zeros((), BF16)) * e_row[None]
        a = a_t[ii * N_KEYS:(ii + 1) * N_KEYS, :]
        act = 0.5 * a * (1.0 + lax.erf(a * (2.0 ** -0.5)))
        hs.append((w * act.astype(BF16).reshape(nk2, PACK16, tm)).reshape(N_KEYS, tm))
    acc_s[...] += _dot(upt_ref[...], jnp.concatenate(hs, axis=0))

    @pl.when(e == pl.num_programs(1) - 1)
    def _():
        y = x1_ref[0] + g2_ref[0] * acc_s[...].T
        ms = jnp.mean(y * y, axis=-1, keepdims=True)
        out_ref[0] = y * lax.rsqrt(ms + EPS) * nf_ref[...]


def _peer(h2t, wq_t, k1, k2, down, up_t, x1, g2, norm_f, tm, ib):
    b, l, d = x1.shape
    nt = l // tm
    ne = N_KEYS // ib
    te = ib * N_KEYS
    const = lambda shp: pl.BlockSpec(shp, lambda t, e: tuple(0 for _ in shp))
    return pl.pallas_call(
        functools.partial(_peer_kernel, ib=ib),
        grid=(b * nt, ne),
        in_specs=[pl.BlockSpec((d, tm), lambda t, e: (0, t)),
                  const((PEER_HEADS * PEER_DQ, d)),
                  const((PEER_HEADS, N_KEYS, PEER_DQ // 2)),
                  const((PEER_HEADS, N_KEYS, PEER_DQ // 2)),
                  pl.BlockSpec((te, d), lambda t, e: (e, 0)),
                  pl.BlockSpec((d, te), lambda t, e: (0, e)),
                  pl.BlockSpec((1, tm, d), lambda t, e: (t // nt, t % nt, 0)),
                  pl.BlockSpec((1, 1, d), lambda t, e: (t // nt, 0, 0)),
                  const((1, d))],
        out_specs=pl.BlockSpec((1, tm, d), lambda t, e: (t // nt, t % nt, 0)),
        out_shape=jax.ShapeDtypeStruct((b, l, d), F32),
        scratch_shapes=[pltpu.VMEM((PEER_HEADS, N_KEYS // PACK16, PACK16, tm), BF16),
                        pltpu.VMEM((PEER_HEADS, N_KEYS // PACK16, PACK16, tm), BF16),
                        pltpu.VMEM((PEER_HEADS, N_KEYS, tm), F32),
                        pltpu.VMEM((PEER_HEADS, N_KEYS, tm), F32),
                        pltpu.VMEM((d, tm), F32)],
        compiler_params=_cparams(("arbitrary", "arbitrary")),
        name="peer",
    )(h2t, wq_t, k1, k2, down, up_t, x1, g2, norm_f)


def _rope_tables(rows, cols):
    r, c = jnp.meshgrid(jnp.arange(rows, dtype=F32), jnp.arange(cols, dtype=F32), indexing="ij")
    n_freq = RET_DK // 4
    inv = ROPE_BASE ** (-jnp.arange(n_freq, dtype=F32) / n_freq)
    ang = jnp.concatenate([r.reshape(-1, 1) * inv, c.reshape(-1, 1) * inv], axis=-1)
    cos, sin = jnp.cos(ang), jnp.sin(ang)
    return jnp.concatenate([cos, cos], axis=-1), jnp.concatenate([-sin, sin], axis=-1)


def kernel(x, c, ctx, c_ctx, w_mod, b_mod, norm1_w, norm2_w, w_in, ret_decay_f, ret_decay_b,
           gla_gk_up_f, gla_gk_bias_f, gla_gk_up_b, gla_gk_bias_b, w_out, peer_w_q, peer_k1, peer_k2,
           peer_down, peer_up, norm_f_w):
    depth = w_mod.shape[0]
    assert depth == 1
    b, l, d = x.shape
    lc = ctx.shape[1]
    li = 0

    cc = jnp.zeros((8, d), F32).at[:b].set(c).at[b].set(c_ctx)
    mod = _modulation(cc, w_mod[li], b_mod[li][None, :])
    sh1, sc1, g1, sh2, sc2, g2 = [mod[:b, j * d:(j + 1) * d][:, None, :] for j in range(N_MOD)]
    csh1, csc1 = [jnp.broadcast_to(mod[b:b + 1, j * d:(j + 1) * d][:, None, :], (b, 1, d)) for j in range(2)]

    w_main = jnp.pad(w_in[li], ((0, 0), (0, MAIN_COLS + LANES - w_in.shape[2]))).astype(BF16)
    gk_up = jnp.zeros((LANES, 2 * GLA_KW), F32)
    gk_up = gk_up.at[:GLA_RANK, :GLA_KW].set(gla_gk_up_f[li])
    gk_up = gk_up.at[GLA_RANK:2 * GLA_RANK, GLA_KW:].set(gla_gk_up_b[li]).astype(BF16)
    gk_bias = jnp.concatenate([gla_gk_bias_f[li], gla_gk_bias_b[li]])[None, :]
    n1 = norm1_w[li][None, :]
    dec_f = jnp.zeros((8, LANES), F32).at[:RET_HEADS].set(jnp.broadcast_to(ret_decay_f[li][:, None], (RET_HEADS, LANES)))
    dec_b = jnp.zeros((8, LANES), F32).at[:RET_HEADS].set(jnp.broadcast_to(ret_decay_b[li][:, None], (RET_HEADS, LANES)))
    cos, sin = _rope_tables(l // GRID_W, GRID_W)
    cos_c = jnp.ones((lc, RET_DK), F32)
    sin_c = jnp.zeros((lc, RET_DK), F32)

    fc = _inproj(ctx, csh1, csc1, n1, w_main, gk_up, gk_bias, cos_c, sin_c, tm=256)
    zr = jnp.zeros((b, RET_HEADS, RET_DK, RET_DK), F32)
    zg = jnp.zeros((b, GLA_HEADS, GLA_DK, GLA_DV), F32)
    ctx_out = _scan((fc[0], fc[1], fc[2], fc[4], fc[5], fc[6], fc[8], fc[9]), dec_f, dec_b,
                    (zr, zr, zg, zg), tb=256)
    ctx_states = ctx_out[4:]

    fl = _inproj(x, sh1, sc1, n1, w_main, gk_up, gk_bias, cos, sin, tm=256)
    lat = _scan((fl[0], fl[1], fl[2], fl[4], fl[5], fl[6], fl[8], fl[9]), dec_f, dec_b,
                ctx_states, tb=256)

    x1, h2t = _mix(lat[0], lat[1], lat[2], lat[3], fl[3], fl[7], x, g1, sh2, sc2,
                   norm2_w[li][None, :], w_out[li].astype(BF16), tm=256)

    wq_t = peer_w_q[li].T.astype(BF16)
    return _peer(h2t, wq_t, peer_k1[li].astype(BF16), peer_k2[li].astype(BF16),
                 peer_down[li].astype(BF16), peer_up[li].T.astype(BF16), x1, g2, norm_f_w[None, :],
                 tm=512, ib=16)
```

```python
import functools

import jax
import jax.numpy as jnp
from jax import lax
from jax.experimental import pallas as pl
from jax.experimental.pallas import tpu as pltpu

F32 = jnp.float32
BF16 = jnp.bfloat16
HIGHEST = lax.Precision.HIGHEST

D_MODEL = 1024
GRID_W = 64
RET_HEADS = 4
RET_WIDTH = 512
RET_DK = 128
GLA_HEADS = 4
GLA_WIDTH = 512
GLA_DV = 128
GLA_DK = 64
GLA_KW = GLA_HEADS * GLA_DK
GLA_RANK = 16
GLA_GATE_NORM = 16.0
CHUNK = 64
LOG2_CHUNK = CHUNK.bit_length() - 1
ROPE_BASE = 10000.0
N_KEYS = 128
PEER_HEADS = 8
PEER_TOPK = 16
PEER_DQ = 256
N_MOD = 6
EPS = 1e-6
MAIN_COLS = 3584
LANES = 128
SUBLANES = 8
PACK16 = 16
MXU_DEPTH = 256
VMEM_LIMIT = 56 * 1024 * 1024


def _cparams(sem):
    return pltpu.CompilerParams(dimension_semantics=sem, vmem_limit_bytes=VMEM_LIMIT)


def _silu(v):
    return v / (1.0 + jnp.exp(-v))


def _log_sigmoid(z):
    return jnp.minimum(z, 0.0) - jnp.log(1.0 + jnp.exp(-jnp.abs(z)))


def _dot(a, b):
    return jnp.dot(a, b, preferred_element_type=F32)


def _dot_nt(a, b):
    return lax.dot_general(a, b, (((1,), (1,)), ((), ())), preferred_element_type=F32)


def _mod_kernel(c_ref, w_ref, b_ref, o_ref):
    a = _silu(c_ref[...])
    o_ref[...] = jnp.dot(a, w_ref[...], precision=HIGHEST, preferred_element_type=F32) + b_ref[...]


def _modulation(cc, w_mod, b_mod):
    n = w_mod.shape[1]
    tn = 1536
    return pl.pallas_call(
        _mod_kernel,
        grid=(n // tn,),
        in_specs=[pl.BlockSpec((8, D_MODEL), lambda j: (0, 0)),
                  pl.BlockSpec((D_MODEL, tn), lambda j: (0, j)),
                  pl.BlockSpec((1, tn), lambda j: (0, j))],
        out_specs=pl.BlockSpec((8, tn), lambda j: (0, j)),
        out_shape=jax.ShapeDtypeStruct((8, n), F32),
        compiler_params=_cparams(("arbitrary",)),
        name="modulation",
    )(cc, w_mod, b_mod)


def _inproj_kernel(x_ref, sh_ref, sc_ref, nw_ref, wm_ref, up_ref, gb_ref, cos_ref, sin_ref,
                   rq_ref, rk_ref, rv_ref, rg_ref, gq_ref, gk_ref, gv_ref, gg_ref, laf_ref, lab_ref):
    x = x_ref[0]
    ms = jnp.mean(x * x, axis=-1, keepdims=True)
    h = x * lax.rsqrt(ms + EPS) * nw_ref[...]
    h = h * (1.0 + sc_ref[0]) + sh_ref[0]
    p = _dot(h.astype(BF16), wm_ref[...])
    cos = cos_ref[...]
    sin = sin_ref[...]
    k_scale = RET_DK ** -0.5
    for hh in range(RET_HEADS):
        lo = hh * RET_DK
        q = p[:, lo:lo + RET_DK]
        k = p[:, RET_WIDTH + lo:RET_WIDTH + lo + RET_DK] * k_scale
        rq_ref[0, :, lo:lo + RET_DK] = q * cos + pltpu.roll(q, RET_DK // 2, axis=1) * sin
        rk_ref[0, :, lo:lo + RET_DK] = k * cos + pltpu.roll(k, RET_DK // 2, axis=1) * sin
    rv_ref[0] = p[:, 1024:1536]
    rg_ref[0] = p[:, 1536:2048]
    gq_ref[0] = p[:, 2048:2304] * (GLA_DK ** -0.5)
    gk_ref[0] = p[:, 2304:2560]
    gv_ref[0] = p[:, 2560:3072]
    gg_ref[0] = p[:, 3072:3584]
    lr = p[:, MAIN_COLS:MAIN_COLS + LANES]
    pre = _dot(lr.astype(BF16), up_ref[...]) + gb_ref[...]
    la = _log_sigmoid(pre) * (1.0 / GLA_GATE_NORM)
    laf_ref[0] = la[:, :GLA_KW]
    lab_ref[0] = la[:, GLA_KW:]


def _inproj(x, shift, scale, norm_w, w_main, gk_up, gk_bias, cos, sin, tm):
    b, l, d = x.shape
    nt = l // tm
    tok = lambda w: pl.BlockSpec((1, tm, w), lambda bi, i: (bi, i, 0))
    full = lambda shp: pl.BlockSpec(shp, lambda bi, i: tuple(0 for _ in shp))
    vec = pl.BlockSpec((1, 1, d), lambda bi, i: (bi, 0, 0))
    widths = (RET_WIDTH, RET_WIDTH, RET_WIDTH, RET_WIDTH, GLA_KW, GLA_KW, GLA_WIDTH, GLA_WIDTH, GLA_KW, GLA_KW)
    return pl.pallas_call(
        _inproj_kernel,
        grid=(b, nt),
        in_specs=[tok(d), vec, vec, full((1, d)), full((d, MAIN_COLS + LANES)),
                  full((LANES, 2 * GLA_KW)), full((1, 2 * GLA_KW)),
                  pl.BlockSpec((tm, RET_DK), lambda bi, i: (i, 0)),
                  pl.BlockSpec((tm, RET_DK), lambda bi, i: (i, 0))],
        out_specs=[tok(w) for w in widths],
        out_shape=[jax.ShapeDtypeStruct((b, l, w), F32) for w in widths],
        compiler_params=_cparams(("arbitrary", "arbitrary")),
        name="inproj",
    )(x, shift, scale, norm_w, w_main, gk_up, gk_bias, cos, sin)


def _scan_kernel(decf_ref, decb_ref,
                 rqf_ref, rkf_ref, rvf_ref, rqb_ref, rkb_ref, rvb_ref,
                 gqf_ref, gkf_ref, gvf_ref, laf_ref, gqb_ref, gkb_ref, gvb_ref, lab_ref,
                 s0rf_ref, s0rb_ref, s0gf_ref, s0gb_ref,
                 orf_ref, orb_ref, ogf_ref, ogb_ref,
                 srf_out, srb_out, sgf_out, sgb_out,
                 srf, srb, sgf, sgb, *, nch):
    i = pl.program_id(1)

    @pl.when(i == 0)
    def _():
        srf[...] = s0rf_ref[0]
        srb[...] = s0rb_ref[0]
        sgf[...] = s0gf_ref[0]
        sgb[...] = s0gb_ref[0]

    C = CHUNK

    lgf = _log_sigmoid(decf_ref[...])
    lgb = _log_sigmoid(decb_ref[...])

    CR = nch * C
    rrow = lax.broadcasted_iota(jnp.int32, (CR, CR), 0)
    rcol = lax.broadcasted_iota(jnp.int32, (CR, CR), 1)
    pos_r = lax.broadcasted_iota(jnp.int32, (CR, LANES), 0).astype(F32)

    def ret_consts(lg, h, fwd):
        lrow = lg[h:h + 1, :]
        lwide = jnp.concatenate([lrow] * (CR // LANES), axis=1)
        rel = ((rrow - rcol) if fwd else (rcol - rrow)).astype(F32)
        mask = jnp.where(rel >= 0, jnp.exp(lwide * jnp.maximum(rel, 0.0)), 0.0)
        if fwd:
            qd = jnp.exp(lrow * (pos_r + 1.0))
            kd = jnp.exp(lrow * (CR - 1.0 - pos_r))
        else:
            qd = jnp.exp(lrow * (CR - pos_r))
            kd = jnp.exp(lrow * pos_r)
        cd = jnp.exp(lrow * float(CR))
        return mask, qd, kd, cd

    def ret_block(q_ref, k_ref, v_ref, o_ref, s_ref, lg, fwd):
        for h in range(RET_HEADS):
            mask, qd, kd, cd = ret_consts(lg, h, fwd)
            lo = h * RET_DK
            q = q_ref[0, :, lo:lo + RET_DK]
            k = k_ref[0, :, lo:lo + RET_DK]
            v = v_ref[0, :, lo:lo + RET_DK].astype(BF16)
            s = s_ref[h]
            scores = _dot_nt(q.astype(BF16), k.astype(BF16)) * mask
            o = _dot(scores.astype(BF16), v) + _dot((q * qd).astype(BF16), s.astype(BF16))
            o_ref[0, :, lo:lo + RET_DK] = o
            s_ref[h] = cd * s + _dot((k * kd).T.astype(BF16), v)

    def gla_block(q_ref, k_ref, v_ref, a_ref, o_ref, s_ref, fwd):
        same = jnp.right_shift(rrow, LOG2_CHUNK) == jnp.right_shift(rcol, LOG2_CHUNK)
        keep = same & ((rrow >= rcol) if fwd else (rcol >= rrow))
        a = a_ref[0]
        g = jnp.dot(jnp.where(keep, 1.0, 0.0).astype(F32), a, precision=HIGHEST,
                    preferred_element_type=F32)
        qt = q_ref[0] * jnp.exp(g)
        kk = k_ref[0]
        kt = kk * jnp.exp(-g)
        kls, tots = [], []
        for c in range(nch):
            g_c = g[c * C:(c + 1) * C, :]
            g_last = g_c[C - 1:C, :] if fwd else g_c[0:1, :]
            kls.append(kk[c * C:(c + 1) * C, :] * jnp.exp(g_last - g_c))
            tots.append(jnp.exp(jnp.sum(a[c * C:(c + 1) * C, :].T, axis=1, keepdims=True)))
        order = range(nch) if fwd else range(nch - 1, -1, -1)
        for h in range(GLA_HEADS):
            lo = h * GLA_DK
            vo = h * GLA_DV
            v = v_ref[0, :, vo:vo + GLA_DV].astype(BF16)
            qh = qt[:, lo:lo + GLA_DK].astype(BF16)
            scores = jnp.where(keep, _dot_nt(qh, kt[:, lo:lo + GLA_DK].astype(BF16)), 0.0)
            o = _dot(scores.astype(BF16), v)
            upd = [_dot(kls[c][:, lo:lo + GLA_DK].T.astype(BF16), v[c * C:(c + 1) * C, :]) for c in range(nch)]
            s = s_ref[h]
            starts = [None] * nch
            for c in order:
                starts[c] = s
                s = tots[c][lo:lo + GLA_DK, :] * s + upd[c]
            s_ref[h] = s
            inter = [_dot(qh[c * C:(c + 1) * C, :], starts[c].astype(BF16)) for c in range(nch)]
            o_ref[0, :, vo:vo + GLA_DV] = o + jnp.concatenate(inter, axis=0)

    ret_block(rqf_ref, rkf_ref, rvf_ref, orf_ref, srf, lgf, True)
    ret_block(rqb_ref, rkb_ref, rvb_ref, orb_ref, srb, lgb, False)
    gla_block(gqf_ref, gkf_ref, gvf_ref, laf_ref, ogf_ref, sgf, True)
    gla_block(gqb_ref, gkb_ref, gvb_ref, lab_ref, ogb_ref, sgb, False)

    @pl.when(i == pl.num_programs(1) - 1)
    def _():
        srf_out[0] = srf[...]
        srb_out[0] = srb[...]
        sgf_out[0] = sgf[...]
        sgb_out[0] = sgb[...]


def _scan(feats, dec_f, dec_b, states, tb):
    rq, rk, rv, gq, gk, gv, la_f, la_b = feats
    b, l, _ = rq.shape
    nb = l // tb
    fwd = lambda w: pl.BlockSpec((1, tb, w), lambda bi, i: (bi, i, 0))
    bwd = lambda w: pl.BlockSpec((1, tb, w), lambda bi, i: (bi, nb - 1 - i, 0))
    dec = pl.BlockSpec((8, LANES), lambda bi, i: (0, 0))
    st_r = pl.BlockSpec((1, RET_HEADS, RET_DK, RET_DK), lambda bi, i: (bi, 0, 0, 0))
    st_g = pl.BlockSpec((1, GLA_HEADS, GLA_DK, GLA_DV), lambda bi, i: (bi, 0, 0, 0))
    o_shape = jax.ShapeDtypeStruct((b, l, RET_WIDTH), F32)
    sr_shape = jax.ShapeDtypeStruct((b, RET_HEADS, RET_DK, RET_DK), F32)
    sg_shape = jax.ShapeDtypeStruct((b, GLA_HEADS, GLA_DK, GLA_DV), F32)
    return pl.pallas_call(
        functools.partial(_scan_kernel, nch=tb // CHUNK),
        grid=(b, nb),
        in_specs=[dec, dec,
                  fwd(RET_WIDTH), fwd(RET_WIDTH), fwd(RET_WIDTH),
                  bwd(RET_WIDTH), bwd(RET_WIDTH), bwd(RET_WIDTH),
                  fwd(GLA_KW), fwd(GLA_KW), fwd(GLA_WIDTH), fwd(GLA_KW),
                  bwd(GLA_KW), bwd(GLA_KW), bwd(GLA_WIDTH), bwd(GLA_KW),
                  st_r, st_r, st_g, st_g],
        out_specs=[fwd(RET_WIDTH), bwd(RET_WIDTH), fwd(GLA_WIDTH), bwd(GLA_WIDTH),
                   st_r, st_r, st_g, st_g],
        out_shape=[o_shape, o_shape, o_shape, o_shape, sr_shape, sr_shape, sg_shape, sg_shape],
        scratch_shapes=[pltpu.VMEM((RET_HEADS, RET_DK, RET_DK), F32),
                        pltpu.VMEM((RET_HEADS, RET_DK, RET_DK), F32),
                        pltpu.VMEM((GLA_HEADS, GLA_DK, GLA_DV), F32),
                        pltpu.VMEM((GLA_HEADS, GLA_DK, GLA_DV), F32)],
        compiler_params=_cparams(("arbitrary", "arbitrary")),
        name="bidir_scan",
    )(dec_f, dec_b, rq, rk, rv, rq, rk, rv, gq, gk, gv, la_f, gq, gk, gv, la_b, *states)


def _mix_kernel(orf_ref, orb_ref, ogf_ref, ogb_ref, rg_ref, gg_ref, x_ref, g1_ref, sh_ref, sc_ref,
                nw_ref, wo_ref, x1_ref, h2t_ref):
    def head_norm(o):
        parts = []
        for h in range(RET_HEADS):
            oh = o[:, h * 128:(h + 1) * 128]
            parts.append(oh * lax.rsqrt(jnp.mean(oh * oh, axis=-1, keepdims=True) + EPS))
        return jnp.concatenate(parts, axis=-1)

    ret = head_norm(orf_ref[0] + orb_ref[0]) * _silu(rg_ref[0])
    gla = head_norm(ogf_ref[0] + ogb_ref[0]) * _silu(gg_ref[0])
    y = jnp.concatenate([ret, gla], axis=-1).astype(BF16)
    x1 = x_ref[0] + g1_ref[0] * _dot(y, wo_ref[...])
    x1_ref[0] = x1
    ms = jnp.mean(x1 * x1, axis=-1, keepdims=True)
    h2 = x1 * lax.rsqrt(ms + EPS) * nw_ref[...]
    h2 = h2 * (1.0 + sc_ref[0]) + sh_ref[0]
    h2t_ref[...] = h2.T.astype(BF16)


def _mix(o_rf, o_rb, o_gf, o_gb, rg, gg, x, g1, sh2, sc2, norm_w, w_out, tm):
    b, l, d = x.shape
    nt = l // tm
    tok = lambda w: pl.BlockSpec((1, tm, w), lambda bi, i: (bi, i, 0))
    vec = pl.BlockSpec((1, 1, d), lambda bi, i: (bi, 0, 0))
    return pl.pallas_call(
        _mix_kernel,
        grid=(b, nt),
        in_specs=[tok(512)] * 6 + [tok(d), vec, vec, vec,
                                   pl.BlockSpec((1, d), lambda bi, i: (0, 0)),
                                   pl.BlockSpec((d, d), lambda bi, i: (0, 0))],
        out_specs=[tok(d), pl.BlockSpec((d, tm), lambda bi, i: (0, bi * nt + i))],
        out_shape=[jax.ShapeDtypeStruct((b, l, d), F32), jax.ShapeDtypeStruct((d, b * l), BF16)],
        compiler_params=_cparams(("arbitrary", "arbitrary")),
        name="mix_outproj",
    )(o_rf, o_rb, o_gf, o_gb, rg, gg, x, g1, sh2, sc2, norm_w, w_out)


def _oddeven_merge_sort_pairs(n):
    pairs = []
    p = 1
    while p < n:
        k = p
        while k >= 1:
            for j in range(k % p, n - k, 2 * k):
                for i in range(min(k, n - j - k)):
                    if (i + j) // (2 * p) == (i + j + k) // (2 * p):
                        pairs.append((i + j, i + j + k))
            k //= 2
        p *= 2
    return pairs


def _bitonic_merge_pairs(n):
    pairs = []
    d = n // 2
    while d >= 1:
        pairs.extend((i, i + d) for i in range(n) if (i & d) == 0)
        d //= 2
    return pairs


_SORT16 = _oddeven_merge_sort_pairs(PEER_TOPK)
_MERGE16 = _bitonic_merge_pairs(PEER_TOPK)


def _compare_exchange(x, pairs):
    for a, b in pairs:
        hi = jnp.maximum(x[a], x[b])
        lo = jnp.minimum(x[a], x[b])
        x[a] = hi
        x[b] = lo


def _sublane_allreduce(v, op):
    for sh in (4, 2, 1):
        v = op(v, pltpu.roll(v, sh, axis=0))
    return v


def _top16_desc(s):
    x = [s[SUBLANES * k:SUBLANES * (k + 1), :] for k in range(PEER_TOPK)]
    _compare_exchange(x, _SORT16)
    for sh in (4, 2, 1):
        c = [jnp.maximum(x[k], pltpu.roll(x[PEER_TOPK - 1 - k], sh, axis=0)) for k in range(PEER_TOPK)]
        _compare_exchange(c, _MERGE16)
        x = c
    return x


def _route_group(s1, s2):
    inf = jnp.inf
    v1 = _top16_desc(s1)
    v2 = _top16_desc(s2)
    sub = lax.broadcasted_iota(jnp.int32, v1[0].shape, 0)

    def pack_rows(v):
        out = v[0]
        for r in range(1, SUBLANES):
            out = jnp.where(sub == r, v[r], out)
        return out

    v1lo, v1hi = pack_rows(v1[:SUBLANES]), pack_rows(v1[SUBLANES:])
    v2lo, v2hi = pack_rows(v2[:SUBLANES]), pack_rows(v2[SUBLANES:])
    cands = [v1[0] + v2lo, v1[0] + v2hi] + [v1[a] + v2lo for a in range(1, SUBLANES)] + [v1hi + v2[0]]
    c = list(cands)
    tau = None
    for _ in range(PEER_TOPK):
        m = c[0]
        for ci in c[1:]:
            m = jnp.maximum(m, ci)
        tau = _sublane_allreduce(m, jnp.maximum)
        c = [jnp.where(ci == tau, -inf, ci) for ci in c]
    m12 = v1[0] + v2[0]
    zs = None
    for cd in cands:
        t = jnp.where(cd >= tau, jnp.exp(cd - m12), 0.0)
        zs = t if zs is None else zs + t
    zinv = 1.0 / _sublane_allreduce(zs, jnp.add)
    thetas = []
    for b in range(PEER_TOPK):
        lo = jnp.where(v1lo + v2[b] >= tau, v1lo, inf)
        hi = jnp.where(v1hi + v2[b] >= tau, v1hi, inf)
        thetas.append(_sublane_allreduce(jnp.minimum(lo, hi), jnp.minimum))
    cnt, e1, rank, e2 = [], [], [], []
    for k in range(N_KEYS // SUBLANES):
        x = s1[SUBLANES * k:SUBLANES * (k + 1), :]
        y = s2[SUBLANES * k:SUBLANES * (k + 1), :]
        ck = jnp.where(x >= thetas[0], 1.0, 0.0)
        rk = jnp.where(v2[0] > y, 1.0, 0.0)
        for b in range(1, PEER_TOPK):
            ck = ck + jnp.where(x >= thetas[b], 1.0, 0.0)
            rk = rk + jnp.where(v2[b] > y, 1.0, 0.0)
        cnt.append(ck)
        rank.append(rk)
        e1.append(jnp.exp(x - v1[0]) * zinv)
        e2.append(jnp.exp(y - v2[0]))
    return cnt, e1, rank, e2


def _peer_kernel(h2t_ref, wqt_ref, k1_ref, k2_ref, down_ref, upt_ref, x1_ref, g2_ref, nf_ref,
                 out_ref,
                 rank2_s, e2_s, cnt_s, e1_s, acc_s, *, ib):
    e = pl.program_id(1)
    tm = h2t_ref.shape[1]
    half = PEER_DQ // 2
    nk2 = N_KEYS // PACK16

    @pl.when(e == 0)
    def _():
        acc_s[...] = jnp.zeros_like(acc_s)

        def head_body(h, carry):
            ht = h2t_ref[...]
            row0 = pl.multiple_of(h * PEER_DQ, PEER_DQ)
            q = _dot(wqt_ref[pl.ds(row0, PEER_DQ), :], ht).astype(BF16)
            s1 = _dot(k1_ref[h], q[:half, :])
            s2 = _dot(k2_ref[h], q[half:, :])
            for g in range(tm // LANES):
                sl = slice(g * LANES, (g + 1) * LANES)
                cnt, e1, rank, e2 = _route_group(s1[:, sl], s2[:, sl])
                cnt_s[h, :, sl] = jnp.concatenate(cnt, axis=0)
                e1_s[h, :, sl] = jnp.concatenate(e1, axis=0)
                for k2 in range(nk2):
                    rank2_s[h, k2, :, sl] = jnp.concatenate(rank[2 * k2:2 * k2 + 2], axis=0).astype(BF16)
                    e2_s[h, k2, :, sl] = jnp.concatenate(e2[2 * k2:2 * k2 + 2], axis=0).astype(BF16)
            return carry

        lax.fori_loop(0, PEER_HEADS, head_body, 0)

    a_t = _dot(down_ref[...], h2t_ref[...])
    hs = []
    for ii in range(ib):
        i = e * ib + ii
        w = jnp.zeros((nk2, PACK16, tm), BF16)
        for h in range(PEER_HEADS):
            c_row = jnp.broadcast_to(cnt_s[h, pl.ds(i, 1), :], (PACK16, tm)).astype(BF16)
            e_row = jnp.broadcast_to(e1_s[h, pl.ds(i, 1), :], (PACK16, tm)).astype(BF16)
            w = w + jnp.where(rank2_s[h] < c_row[None], e2_s[h], jnp.zeros((), BF16)) * e_row[None]
        a = a_t[ii * N_KEYS:(ii + 1) * N_KEYS, :]
        act = 0.5 * a * (1.0 + lax.erf(a * (2.0 ** -0.5)))
        hs.append((w * act.astype(BF16).reshape(nk2, PACK16, tm)).reshape(N_KEYS, tm))
    acc_s[...] += _dot(upt_ref[...], jnp.concatenate(hs, axis=0))

    @pl.when(e == pl.num_programs(1) - 1)
    def _():
        y = x1_ref[0] + g2_ref[0] * acc_s[...].T
        ms = jnp.mean(y * y, axis=-1, keepdims=True)
        out_ref[0] = y * lax.rsqrt(ms + EPS) * nf_ref[...]


def _peer(h2t, wq_t, k1, k2, down, up_t, x1, g2, norm_f, tm, ib):
    b, l, d = x1.shape
    nt = l // tm
    ne = N_KEYS // ib
    te = ib * N_KEYS
    const = lambda shp: pl.BlockSpec(shp, lambda t, e: tuple(0 for _ in shp))
    return pl.pallas_call(
        functools.partial(_peer_kernel, ib=ib),
        grid=(b * nt, ne),
        in_specs=[pl.BlockSpec((d, tm), lambda t, e: (0, t)),
                  const((PEER_HEADS * PEER_DQ, d)),
                  const((PEER_HEADS, N_KEYS, PEER_DQ // 2)),
                  const((PEER_HEADS, N_KEYS, PEER_DQ // 2)),
                  pl.BlockSpec((te, d), lambda t, e: (e, 0)),
                  pl.BlockSpec((d, te), lambda t, e: (0, e)),
                  pl.BlockSpec((1, tm, d), lambda t, e: (t // nt, t % nt, 0)),
                  pl.BlockSpec((1, 1, d), lambda t, e: (t // nt, 0, 0)),
                  const((1, d))],
        out_specs=pl.BlockSpec((1, tm, d), lambda t, e: (t // nt, t % nt, 0)),
        out_shape=jax.ShapeDtypeStruct((b, l, d), F32),
        scratch_shapes=[pltpu.VMEM((PEER_HEADS, N_KEYS // PACK16, PACK16, tm), BF16),
                        pltpu.VMEM((PEER_HEADS, N_KEYS // PACK16, PACK16, tm), BF16),
                        pltpu.VMEM((PEER_HEADS, N_KEYS, tm), F32),
                        pltpu.VMEM((PEER_HEADS, N_KEYS, tm), F32),
                        pltpu.VMEM((d, tm), F32)],
        compiler_params=_cparams(("arbitrary", "arbitrary")),
        name="peer",
    )(h2t, wq_t, k1, k2, down, up_t, x1, g2, norm_f)


def _rope_tables(rows, cols):
    r, c = jnp.meshgrid(jnp.arange(rows, dtype=F32), jnp.arange(cols, dtype=F32), indexing="ij")
    n_freq = RET_DK // 4
    inv = ROPE_BASE ** (-jnp.arange(n_freq, dtype=F32) / n_freq)
    ang = jnp.concatenate([r.reshape(-1, 1) * inv, c.reshape(-1, 1) * inv], axis=-1)
    cos, sin = jnp.cos(ang), jnp.sin(ang)
    return jnp.concatenate([cos, cos], axis=-1), jnp.concatenate([-sin, sin], axis=-1)


def kernel(x, c, ctx, c_ctx, w_mod, b_mod, norm1_w, norm2_w, w_in, ret_decay_f, ret_decay_b,
           gla_gk_up_f, gla_gk_bias_f, gla_gk_up_b, gla_gk_bias_b, w_out, peer_w_q, peer_k1, peer_k2,
           peer_down, peer_up, norm_f_w):
    depth = w_mod.shape[0]
    assert depth == 1
    b, l, d = x.shape
    lc = ctx.shape[1]
    li = 0

    cc = jnp.zeros((8, d), F32).at[:b].set(c).at[b].set(c_ctx)
    mod = _modulation(cc, w_mod[li], b_mod[li][None, :])
    sh1, sc1, g1, sh2, sc2, g2 = [mod[:b, j * d:(j + 1) * d][:, None, :] for j in range(N_MOD)]
    csh1, csc1 = [jnp.broadcast_to(mod[b:b + 1, j * d:(j + 1) * d][:, None, :], (b, 1, d)) for j in range(2)]

    w_main = jnp.pad(w_in[li], ((0, 0), (0, MAIN_COLS + LANES - w_in.shape[2]))).astype(BF16)
    gk_up = jnp.zeros((LANES, 2 * GLA_KW), F32)
    gk_up = gk_up.at[:GLA_RANK, :GLA_KW].set(gla_gk_up_f[li])
    gk_up = gk_up.at[GLA_RANK:2 * GLA_RANK, GLA_KW:].set(gla_gk_up_b[li]).astype(BF16)
    gk_bias = jnp.concatenate([gla_gk_bias_f[li], gla_gk_bias_b[li]])[None, :]
    n1 = norm1_w[li][None, :]
    dec_f = jnp.zeros((8, LANES), F32).at[:RET_HEADS].set(jnp.broadcast_to(ret_decay_f[li][:, None], (RET_HEADS, LANES)))
    dec_b = jnp.zeros((8, LANES), F32).at[:RET_HEADS].set(jnp.broadcast_to(ret_decay_b[li][:, None], (RET_HEADS, LANES)))
    cos, sin = _rope_tables(l // GRID_W, GRID_W)
    cos_c = jnp.ones((lc, RET_DK), F32)
    sin_c = jnp.zeros((lc, RET_DK), F32)

    fc = _inproj(ctx, csh1, csc1, n1, w_main, gk_up, gk_bias, cos_c, sin_c, tm=256)
    zr = jnp.zeros((b, RET_HEADS, RET_DK, RET_DK), F32)
    zg = jnp.zeros((b, GLA_HEADS, GLA_DK, GLA_DV), F32)
    ctx_out = _scan((fc[0], fc[1], fc[2], fc[4], fc[5], fc[6], fc[8], fc[9]), dec_f, dec_b,
                    (zr, zr, zg, zg), tb=256)
    ctx_states = ctx_out[4:]

    fl = _inproj(x, sh1, sc1, n1, w_main, gk_up, gk_bias, cos, sin, tm=256)
    lat = _scan((fl[0], fl[1], fl[2], fl[4], fl[5], fl[6], fl[8], fl[9]), dec_f, dec_b,
                ctx_states, tb=256)

    x1, h2t = _mix(lat[0], lat[1], lat[2], lat[3], fl[3], fl[7], x, g1, sh2, sc2,
                   norm2_w[li][None, :], w_out[li].astype(BF16), tm=256)

    wq_t = peer_w_q[li].T.astype(BF16)
    return _peer(h2t, wq_t, peer_k1[li].astype(BF16), peer_k2[li].astype(BF16),
                 peer_down[li].astype(BF16), peer_up[li].T.astype(BF16), x1, g2, norm_f_w[None, :],
                 tm=512, ib=16)
```

```python
import functools

import jax
import jax.numpy as jnp
from jax import lax
from jax.experimental import pallas as pl
from jax.experimental.pallas import tpu as pltpu

F32 = jnp.float32
BF16 = jnp.bfloat16
HIGHEST = lax.Precision.HIGHEST

D_MODEL = 1024
GRID_W = 64
RET_HEADS = 4
RET_WIDTH = 512
RET_DK = 128
GLA_HEADS = 4
GLA_WIDTH = 512
GLA_DV = 128
GLA_DK = 64
GLA_KW = GLA_HEADS * GLA_DK
GLA_RANK = 16
GLA_GATE_NORM = 16.0
CHUNK = 64
LOG2_CHUNK = CHUNK.bit_length() - 1
ROPE_BASE = 10000.0
N_KEYS = 128
PEER_HEADS = 8
PEER_TOPK = 16
PEER_DQ = 256
N_MOD = 6
EPS = 1e-6
MAIN_COLS = 3584
LANES = 128
SUBLANES = 8
PACK16 = 16
MXU_DEPTH = 256
VMEM_LIMIT = 56 * 1024 * 1024


def _cparams(sem):
    return pltpu.CompilerParams(dimension_semantics=sem, vmem_limit_bytes=VMEM_LIMIT)


def _silu(v):
    return v / (1.0 + jnp.exp(-v))


def _log_sigmoid(z):
    return jnp.minimum(z, 0.0) - jnp.log(1.0 + jnp.exp(-jnp.abs(z)))


def _dot(a, b):
    return jnp.dot(a, b, preferred_element_type=F32)


def _dot_nt(a, b):
    return lax.dot_general(a, b, (((1,), (1,)), ((), ())), preferred_element_type=F32)


def _mod_kernel(c_ref, w_ref, b_ref, o_ref):
    a = _silu(c_ref[...])
    o_ref[...] = jnp.dot(a, w_ref[...], precision=HIGHEST, preferred_element_type=F32) + b_ref[...]


def _modulation(cc, w_mod, b_mod):
    n = w_mod.shape[1]
    tn = 1536
    return pl.pallas_call(
        _mod_kernel,
        grid=(n // tn,),
        in_specs=[pl.BlockSpec((8, D_MODEL), lambda j: (0, 0)),
                  pl.BlockSpec((D_MODEL, tn), lambda j: (0, j)),
                  pl.BlockSpec((1, tn), lambda j: (0, j))],
        out_specs=pl.BlockSpec((8, tn), lambda j: (0, j)),
        out_shape=jax.ShapeDtypeStruct((8, n), F32),
        compiler_params=_cparams(("arbitrary",)),
        name="modulation",
    )(cc, w_mod, b_mod)


def _inproj_kernel(x_ref, sh_ref, sc_ref, nw_ref, wm_ref, up_ref, gb_ref, cos_ref, sin_ref,
                   rq_ref, rk_ref, rv_ref, rg_ref, gq_ref, gk_ref, gv_ref, gg_ref, laf_ref, lab_ref):
    x = x_ref[0]
    ms = jnp.mean(x * x, axis=-1, keepdims=True)
    h = x * lax.rsqrt(ms + EPS) * nw_ref[...]
    h = h * (1.0 + sc_ref[0]) + sh_ref[0]
    p = _dot(h.astype(BF16), wm_ref[...])
    cos = cos_ref[...]
    sin = sin_ref[...]
    k_scale = RET_DK ** -0.5
    for hh in range(RET_HEADS):
        lo = hh * RET_DK
        q = p[:, lo:lo + RET_DK]
        k = p[:, RET_WIDTH + lo:RET_WIDTH + lo + RET_DK] * k_scale
        rq_ref[0, :, lo:lo + RET_DK] = q * cos + pltpu.roll(q, RET_DK // 2, axis=1) * sin
        rk_ref[0, :, lo:lo + RET_DK] = k * cos + pltpu.roll(k, RET_DK // 2, axis=1) * sin
    rv_ref[0] = p[:, 1024:1536]
    rg_ref[0] = p[:, 1536:2048]
    gq_ref[0] = p[:, 2048:2304] * (GLA_DK ** -0.5)
    gk_ref[0] = p[:, 2304:2560]
    gv_ref[0] = p[:, 2560:3072]
    gg_ref[0] = p[:, 3072:3584]
    lr = p[:, MAIN_COLS:MAIN_COLS + LANES]
    pre = _dot(lr.astype(BF16), up_ref[...]) + gb_ref[...]
    la = _log_sigmoid(pre) * (1.0 / GLA_GATE_NORM)
    laf_ref[0] = la[:, :GLA_KW]
    lab_ref[0] = la[:, GLA_KW:]


def _inproj(x, shift, scale, norm_w, w_main, gk_up, gk_bias, cos, sin, tm):
    b, l, d = x.shape
    nt = l // tm
    tok = lambda w: pl.BlockSpec((1, tm, w), lambda bi, i: (bi, i, 0))
    full = lambda shp: pl.BlockSpec(shp, lambda bi, i: tuple(0 for _ in shp))
    vec = pl.BlockSpec((1, 1, d), lambda bi, i: (bi, 0, 0))
    widths = (RET_WIDTH, RET_WIDTH, RET_WIDTH, RET_WIDTH, GLA_KW, GLA_KW, GLA_WIDTH, GLA_WIDTH, GLA_KW, GLA_KW)
    return pl.pallas_call(
        _inproj_kernel,
        grid=(b, nt),
        in_specs=[tok(d), vec, vec, full((1, d)), full((d, MAIN_COLS + LANES)),
                  full((LANES, 2 * GLA_KW)), full((1, 2 * GLA_KW)),
                  pl.BlockSpec((tm, RET_DK), lambda bi, i: (i, 0)),
                  pl.BlockSpec((tm, RET_DK), lambda bi, i: (i, 0))],
        out_specs=[tok(w) for w in widths],
        out_shape=[jax.ShapeDtypeStruct((b, l, w), F32) for w in widths],
        compiler_params=_cparams(("arbitrary", "arbitrary")),
        name="inproj",
    )(x, shift, scale, norm_w, w_main, gk_up, gk_bias, cos, sin)


def _scan_kernel(decf_ref, decb_ref,
                 rqf_ref, rkf_ref, rvf_ref, rqb_ref, rkb_ref, rvb_ref,
                 gqf_ref, gkf_ref, gvf_ref, laf_ref, gqb_ref, gkb_ref, gvb_ref, lab_ref,
                 s0rf_ref, s0rb_ref, s0gf_ref, s0gb_ref,
                 orf_ref, orb_ref, ogf_ref, ogb_ref,
                 srf_out, srb_out, sgf_out, sgb_out,
                 srf, srb, sgf, sgb, *, nch):
    i = pl.program_id(1)

    @pl.when(i == 0)
    def _():
        srf[...] = s0rf_ref[0]
        srb[...] = s0rb_ref[0]
        sgf[...] = s0gf_ref[0]
        sgb[...] = s0gb_ref[0]

    C = CHUNK

    lgf = _log_sigmoid(decf_ref[...])
    lgb = _log_sigmoid(decb_ref[...])

    CR = nch * C
    rrow = lax.broadcasted_iota(jnp.int32, (CR, CR), 0)
    rcol = lax.broadcasted_iota(jnp.int32, (CR, CR), 1)
    pos_r = lax.broadcasted_iota(jnp.int32, (CR, LANES), 0).astype(F32)

    def ret_consts(lg, h, fwd):
        lrow = lg[h:h + 1, :]
        lwide = jnp.concatenate([lrow] * (CR // LANES), axis=1)
        rel = ((rrow - rcol) if fwd else (rcol - rrow)).astype(F32)
        mask = jnp.where(rel >= 0, jnp.exp(lwide * jnp.maximum(rel, 0.0)), 0.0)
        if fwd:
            qd = jnp.exp(lrow * (pos_r + 1.0))
            kd = jnp.exp(lrow * (CR - 1.0 - pos_r))
        else:
            qd = jnp.exp(lrow * (CR - pos_r))
            kd = jnp.exp(lrow * pos_r)
        cd = jnp.exp(lrow * float(CR))
        return mask, qd, kd, cd

    def ret_block(q_ref, k_ref, v_ref, o_ref, s_ref, lg, fwd):
        for h in range(RET_HEADS):
            mask, qd, kd, cd = ret_consts(lg, h, fwd)
            lo = h * RET_DK
            q = q_ref[0, :, lo:lo + RET_DK]
            k = k_ref[0, :, lo:lo + RET_DK]
            v = v_ref[0, :, lo:lo + RET_DK].astype(BF16)
            s = s_ref[h]
            scores = _dot_nt(q.astype(BF16), k.astype(BF16)) * mask
            o = _dot(scores.astype(BF16), v) + _dot((q * qd).astype(BF16), s.astype(BF16))
            o_ref[0, :, lo:lo + RET_DK] = o.astype(o_ref.dtype)
            s_ref[h] = cd * s + _dot((k * kd).T.astype(BF16), v)

    def gla_block(q_ref, k_ref, v_ref, a_ref, o_ref, s_ref, fwd):
        same = jnp.right_shift(rrow, LOG2_CHUNK) == jnp.right_shift(rcol, LOG2_CHUNK)
        keep = same & ((rrow >= rcol) if fwd else (rcol >= rrow))
        a = a_ref[0]
        g = jnp.dot(jnp.where(keep, 1.0, 0.0).astype(F32), a, precision=HIGHEST,
                    preferred_element_type=F32)
        qt = q_ref[0] * jnp.exp(g)
        kk = k_ref[0]
        kt = kk * jnp.exp(-g)
        kls, tots = [], []
        for c in range(nch):
            g_c = g[c * C:(c + 1) * C, :]
            g_last = g_c[C - 1:C, :] if fwd else g_c[0:1, :]
            kls.append(kk[c * C:(c + 1) * C, :] * jnp.exp(g_last - g_c))
            tots.append(jnp.exp(jnp.sum(a[c * C:(c + 1) * C, :].T, axis=1, keepdims=True)))
        order = range(nch) if fwd else range(nch - 1, -1, -1)
        for h in range(GLA_HEADS):
            lo = h * GLA_DK
            vo = h * GLA_DV
            v = v_ref[0, :, vo:vo + GLA_DV].astype(BF16)
            qh = qt[:, lo:lo + GLA_DK].astype(BF16)
            scores = jnp.where(keep, _dot_nt(qh, kt[:, lo:lo + GLA_DK].astype(BF16)), 0.0)
            o = _dot(scores.astype(BF16), v)
            upd = [_dot(kls[c][:, lo:lo + GLA_DK].T.astype(BF16), v[c * C:(c + 1) * C, :]) for c in range(nch)]
            s = s_ref[h]
            starts = [None] * nch
            for c in order:
                starts[c] = s
                s = tots[c][lo:lo + GLA_DK, :] * s + upd[c]
            s_ref[h] = s
            inter = [_dot(qh[c * C:(c + 1) * C, :], starts[c].astype(BF16)) for c in range(nch)]
            o_ref[0, :, vo:vo + GLA_DV] = (o + jnp.concatenate(inter, axis=0)).astype(o_ref.dtype)

    ret_block(rqf_ref, rkf_ref, rvf_ref, orf_ref, srf, lgf, True)
    ret_block(rqb_ref, rkb_ref, rvb_ref, orb_ref, srb, lgb, False)
    gla_block(gqf_ref, gkf_ref, gvf_ref, laf_ref, ogf_ref, sgf, True)
    gla_block(gqb_ref, gkb_ref, gvb_ref, lab_ref, ogb_ref, sgb, False)

    @pl.when(i == pl.num_programs(1) - 1)
    def _():
        srf_out[0] = srf[...]
        srb_out[0] = srb[...]
        sgf_out[0] = sgf[...]
        sgb_out[0] = sgb[...]


def _scan(feats, dec_f, dec_b, states, tb):
    rq, rk, rv, gq, gk, gv, la_f, la_b = feats
    b, l, _ = rq.shape
    nb = l // tb
    fwd = lambda w: pl.BlockSpec((1, tb, w), lambda bi, i: (bi, i, 0))
    bwd = lambda w: pl.BlockSpec((1, tb, w), lambda bi, i: (bi, nb - 1 - i, 0))
    dec = pl.BlockSpec((8, LANES), lambda bi, i: (0, 0))
    st_r = pl.BlockSpec((1, RET_HEADS, RET_DK, RET_DK), lambda bi, i: (bi, 0, 0, 0))
    st_g = pl.BlockSpec((1, GLA_HEADS, GLA_DK, GLA_DV), lambda bi, i: (bi, 0, 0, 0))
    o_shape = jax.ShapeDtypeStruct((b, l, RET_WIDTH), BF16)
    sr_shape = jax.ShapeDtypeStruct((b, RET_HEADS, RET_DK, RET_DK), F32)
    sg_shape = jax.ShapeDtypeStruct((b, GLA_HEADS, GLA_DK, GLA_DV), F32)
    return pl.pallas_call(
        functools.partial(_scan_kernel, nch=tb // CHUNK),
        grid=(b, nb),
        in_specs=[dec, dec,
                  fwd(RET_WIDTH), fwd(RET_WIDTH), fwd(RET_WIDTH),
                  bwd(RET_WIDTH), bwd(RET_WIDTH), bwd(RET_WIDTH),
                  fwd(GLA_KW), fwd(GLA_KW), fwd(GLA_WIDTH), fwd(GLA_KW),
                  bwd(GLA_KW), bwd(GLA_KW), bwd(GLA_WIDTH), bwd(GLA_KW),
                  st_r, st_r, st_g, st_g],
        out_specs=[fwd(RET_WIDTH), bwd(RET_WIDTH), fwd(GLA_WIDTH), bwd(GLA_WIDTH),
                   st_r, st_r, st_g, st_g],
        out_shape=[o_shape, o_shape, o_shape, o_shape, sr_shape, sr_shape, sg_shape, sg_shape],
        scratch_shapes=[pltpu.VMEM((RET_HEADS, RET_DK, RET_DK), F32),
                        pltpu.VMEM((RET_HEADS, RET_DK, RET_DK), F32),
                        pltpu.VMEM((GLA_HEADS, GLA_DK, GLA_DV), F32),
                        pltpu.VMEM((GLA_HEADS, GLA_DK, GLA_DV), F32)],
        compiler_params=_cparams(("arbitrary", "arbitrary")),
        name="bidir_scan",
    )(dec_f, dec_b, rq, rk, rv, rq, rk, rv, gq, gk, gv, la_f, gq, gk, gv, la_b, *states)


def _mix_kernel(orf_ref, orb_ref, ogf_ref, ogb_ref, rg_ref, gg_ref, x_ref, g1_ref, sh_ref, sc_ref,
                nw_ref, wo_ref, x1_ref, h2t_ref):
    def head_norm(o):
        parts = []
        for h in range(RET_HEADS):
            oh = o[:, h * 128:(h + 1) * 128]
            parts.append(oh * lax.rsqrt(jnp.mean(oh * oh, axis=-1, keepdims=True) + EPS))
        return jnp.concatenate(parts, axis=-1)

    ret = head_norm(orf_ref[0].astype(F32) + orb_ref[0].astype(F32)) * _silu(rg_ref[0])
    gla = head_norm(ogf_ref[0].astype(F32) + ogb_ref[0].astype(F32)) * _silu(gg_ref[0])
    y = jnp.concatenate([ret, gla], axis=-1).astype(BF16)
    x1 = x_ref[0] + g1_ref[0] * _dot(y, wo_ref[...])
    x1_ref[0] = x1
    ms = jnp.mean(x1 * x1, axis=-1, keepdims=True)
    h2 = x1 * lax.rsqrt(ms + EPS) * nw_ref[...]
    h2 = h2 * (1.0 + sc_ref[0]) + sh_ref[0]
    h2t_ref[...] = h2.T.astype(BF16)


def _mix(o_rf, o_rb, o_gf, o_gb, rg, gg, x, g1, sh2, sc2, norm_w, w_out, tm):
    b, l, d = x.shape
    nt = l // tm
    tok = lambda w: pl.BlockSpec((1, tm, w), lambda bi, i: (bi, i, 0))
    vec = pl.BlockSpec((1, 1, d), lambda bi, i: (bi, 0, 0))
    return pl.pallas_call(
        _mix_kernel,
        grid=(b, nt),
        in_specs=[tok(512)] * 6 + [tok(d), vec, vec, vec,
                                   pl.BlockSpec((1, d), lambda bi, i: (0, 0)),
                                   pl.BlockSpec((d, d), lambda bi, i: (0, 0))],
        out_specs=[tok(d), pl.BlockSpec((d, tm), lambda bi, i: (0, bi * nt + i))],
        out_shape=[jax.ShapeDtypeStruct((b, l, d), F32), jax.ShapeDtypeStruct((d, b * l), BF16)],
        compiler_params=_cparams(("arbitrary", "arbitrary")),
        name="mix_outproj",
    )(o_rf, o_rb, o_gf, o_gb, rg, gg, x, g1, sh2, sc2, norm_w, w_out)


def _oddeven_merge_sort_pairs(n):
    pairs = []
    p = 1
    while p < n:
        k = p
        while k >= 1:
            for j in range(k % p, n - k, 2 * k):
                for i in range(min(k, n - j - k)):
                    if (i + j) // (2 * p) == (i + j + k) // (2 * p):
                        pairs.append((i + j, i + j + k))
            k //= 2
        p *= 2
    return pairs


def _bitonic_merge_pairs(n):
    pairs = []
    d = n // 2
    while d >= 1:
        pairs.extend((i, i + d) for i in range(n) if (i & d) == 0)
        d //= 2
    return pairs


_SORT16 = _oddeven_merge_sort_pairs(PEER_TOPK)
_MERGE16 = _bitonic_merge_pairs(PEER_TOPK)


def _compare_exchange(x, pairs):
    for a, b in pairs:
        hi = jnp.maximum(x[a], x[b])
        lo = jnp.minimum(x[a], x[b])
        x[a] = hi
        x[b] = lo


def _sublane_allreduce(v, op):
    for sh in (4, 2, 1):
        v = op(v, pltpu.roll(v, sh, axis=0))
    return v


def _top16_desc(s):
    x = [s[SUBLANES * k:SUBLANES * (k + 1), :] for k in range(PEER_TOPK)]
    _compare_exchange(x, _SORT16)
    for sh in (4, 2, 1):
        c = [jnp.maximum(x[k], pltpu.roll(x[PEER_TOPK - 1 - k], sh, axis=0)) for k in range(PEER_TOPK)]
        _compare_exchange(c, _MERGE16)
        x = c
    return x


def _route_group(s1, s2):
    inf = jnp.inf
    v1 = _top16_desc(s1)
    v2 = _top16_desc(s2)
    sub = lax.broadcasted_iota(jnp.int32, v1[0].shape, 0)

    def pack_rows(v):
        out = v[0]
        for r in range(1, SUBLANES):
            out = jnp.where(sub == r, v[r], out)
        return out

    v1lo, v1hi = pack_rows(v1[:SUBLANES]), pack_rows(v1[SUBLANES:])
    v2lo, v2hi = pack_rows(v2[:SUBLANES]), pack_rows(v2[SUBLANES:])
    cands = [v1[0] + v2lo, v1[0] + v2hi] + [v1[a] + v2lo for a in range(1, SUBLANES)] + [v1hi + v2[0]]
    c = list(cands)
    tau = None
    for _ in range(PEER_TOPK):
        m = c[0]
        for ci in c[1:]:
            m = jnp.maximum(m, ci)
        tau = _sublane_allreduce(m, jnp.maximum)
        c = [jnp.where(ci == tau, -inf, ci) for ci in c]
    m12 = v1[0] + v2[0]
    zs = None
    for cd in cands:
        t = jnp.where(cd >= tau, jnp.exp(cd - m12), 0.0)
        zs = t if zs is None else zs + t
    zinv = 1.0 / _sublane_allreduce(zs, jnp.add)
    thetas = []
    for b in range(PEER_TOPK):
        lo = jnp.where(v1lo + v2[b] >= tau, v1lo, inf)
        hi = jnp.where(v1hi + v2[b] >= tau, v1hi, inf)
        thetas.append(_sublane_allreduce(jnp.minimum(lo, hi), jnp.minimum))
    cnt, e1, rank, e2 = [], [], [], []
    for k in range(N_KEYS // SUBLANES):
        x = s1[SUBLANES * k:SUBLANES * (k + 1), :]
        y = s2[SUBLANES * k:SUBLANES * (k + 1), :]
        ck = jnp.where(x >= thetas[0], 1.0, 0.0)
        rk = jnp.where(v2[0] > y, 1.0, 0.0)
        for b in range(1, PEER_TOPK):
            ck = ck + jnp.where(x >= thetas[b], 1.0, 0.0)
            rk = rk + jnp.where(v2[b] > y, 1.0, 0.0)
        cnt.append(ck)
        rank.append(rk)
        e1.append(jnp.exp(x - v1[0]) * zinv)
        e2.append(jnp.exp(y - v2[0]))
    return cnt, e1, rank, e2


def _peer_kernel(h2t_ref, wqt_ref, k1_ref, k2_ref, down_ref, upt_ref, x1_ref, g2_ref, nf_ref,
                 out_ref,
                 rank2_s, e2_s, cnt_s, e1_s, acc_s, *, ib):
    e = pl.program_id(1)
    tm = h2t_ref.shape[1]
    half = PEER_DQ // 2
    nk2 = N_KEYS // PACK16

    @pl.when(e == 0)
    def _():
        acc_s[...] = jnp.zeros_like(acc_s)

        def head_body(h, carry):
            ht = h2t_ref[...]
            row0 = pl.multiple_of(h * PEER_DQ, PEER_DQ)
            q = _dot(wqt_ref[pl.ds(row0, PEER_DQ), :], ht).astype(BF16)
            s1 = _dot(k1_ref[h], q[:half, :])
            s2 = _dot(k2_ref[h], q[half:, :])
            for g in range(tm // LANES):
                sl = slice(g * LANES, (g + 1) * LANES)
                cnt, e1, rank, e2 = _route_group(s1[:, sl], s2[:, sl])
                cnt_s[h, :, sl] = jnp.concatenate(cnt, axis=0)
                e1_s[h, :, sl] = jnp.concatenate(e1, axis=0)
                for k2 in range(nk2):
                    rank2_s[h, k2, :, sl] = jnp.concatenate(rank[2 * k2:2 * k2 + 2], axis=0).astype(BF16)
                    e2_s[h, k2, :, sl] = jnp.concatenate(e2[2 * k2:2 * k2 + 2], axis=0).astype(BF16)
            return carry

        lax.fori_loop(0, PEER_HEADS, head_body, 0)

    a_t = _dot(down_ref[...], h2t_ref[...])
    hs = []
    for ii in range(ib):
        i = e * ib + ii
        w = jnp.zeros((nk2, PACK16, tm), BF16)
        for h in range(PEER_HEADS):
            c_row = jnp.broadcast_to(cnt_s[h, pl.ds(i, 1), :], (PACK16, tm)).astype(BF16)
            e_row = jnp.broadcast_to(e1_s[h, pl.ds(i, 1), :], (PACK16, tm)).astype(BF16)
            w = w + jnp.where(rank2_s[h] < c_row[None], e2_s[h], jnp.zeros((), BF16)) * e_row[None]
        a = a_t[ii * N_KEYS:(ii + 1) * N_KEYS, :]
        act = 0.5 * a * (1.0 + lax.erf(a * (2.0 ** -0.5)))
        hs.append((w * act.astype(BF16).reshape(nk2, PACK16, tm)).reshape(N_KEYS, tm))
    acc_s[...] += _dot(upt_ref[...], jnp.concatenate(hs, axis=0))

    @pl.when(e == pl.num_programs(1) - 1)
    def _():
        y = x1_ref[0] + g2_ref[0] * acc_s[...].T
        ms = jnp.mean(y * y, axis=-1, keepdims=True)
        out_ref[0] = y * lax.rsqrt(ms + EPS) * nf_ref[...]


def _peer(h2t, wq_t, k1, k2, down, up_t, x1, g2, norm_f, tm, ib):
    b, l, d = x1.shape
    nt = l // tm
    ne = N_KEYS // ib
    te = ib * N_KEYS
    const = lambda shp: pl.BlockSpec(shp, lambda t, e: tuple(0 for _ in shp))
    return pl.pallas_call(
        functools.partial(_peer_kernel, ib=ib),
        grid=(b * nt, ne),
        in_specs=[pl.BlockSpec((d, tm), lambda t, e: (0, t)),
                  const((PEER_HEADS * PEER_DQ, d)),
                  const((PEER_HEADS, N_KEYS, PEER_DQ // 2)),
                  const((PEER_HEADS, N_KEYS, PEER_DQ // 2)),
                  pl.BlockSpec((te, d), lambda t, e: (e, 0)),
                  pl.BlockSpec((d, te), lambda t, e: (0, e)),
                  pl.BlockSpec((1, tm, d), lambda t, e: (t // nt, t % nt, 0)),
                  pl.BlockSpec((1, 1, d), lambda t, e: (t // nt, 0, 0)),
                  const((1, d))],
        out_specs=pl.BlockSpec((1, tm, d), lambda t, e: (t // nt, t % nt, 0)),
        out_shape=jax.ShapeDtypeStruct((b, l, d), F32),
        scratch_shapes=[pltpu.VMEM((PEER_HEADS, N_KEYS // PACK16, PACK16, tm), BF16),
                        pltpu.VMEM((PEER_HEADS, N_KEYS // PACK16, PACK16, tm), BF16),
                        pltpu.VMEM((PEER_HEADS, N_KEYS, tm), F32),
                        pltpu.VMEM((PEER_HEADS, N_KEYS, tm), F32),
                        pltpu.VMEM((d, tm), F32)],
        compiler_params=_cparams(("arbitrary", "arbitrary")),
        name="peer",
    )(h2t, wq_t, k1, k2, down, up_t, x1, g2, norm_f)


def _rope_tables(rows, cols):
    r, c = jnp.meshgrid(jnp.arange(rows, dtype=F32), jnp.arange(cols, dtype=F32), indexing="ij")
    n_freq = RET_DK // 4
    inv = ROPE_BASE ** (-jnp.arange(n_freq, dtype=F32) / n_freq)
    ang = jnp.concatenate([r.reshape(-1, 1) * inv, c.reshape(-1, 1) * inv], axis=-1)
    cos, sin = jnp.cos(ang), jnp.sin(ang)
    return jnp.concatenate([cos, cos], axis=-1), jnp.concatenate([-sin, sin], axis=-1)


def kernel(x, c, ctx, c_ctx, w_mod, b_mod, norm1_w, norm2_w, w_in, ret_decay_f, ret_decay_b,
           gla_gk_up_f, gla_gk_bias_f, gla_gk_up_b, gla_gk_bias_b, w_out, peer_w_q, peer_k1, peer_k2,
           peer_down, peer_up, norm_f_w):
    depth = w_mod.shape[0]
    assert depth == 1
    b, l, d = x.shape
    lc = ctx.shape[1]
    li = 0

    cc = jnp.zeros((8, d), F32).at[:b].set(c).at[b].set(c_ctx)
    mod = _modulation(cc, w_mod[li], b_mod[li][None, :])
    sh1, sc1, g1, sh2, sc2, g2 = [mod[:b, j * d:(j + 1) * d][:, None, :] for j in range(N_MOD)]
    csh1, csc1 = [jnp.broadcast_to(mod[b:b + 1, j * d:(j + 1) * d][:, None, :], (b, 1, d)) for j in range(2)]

    w_main = jnp.pad(w_in[li], ((0, 0), (0, MAIN_COLS + LANES - w_in.shape[2]))).astype(BF16)
    gk_up = jnp.zeros((LANES, 2 * GLA_KW), F32)
    gk_up = gk_up.at[:GLA_RANK, :GLA_KW].set(gla_gk_up_f[li])
    gk_up = gk_up.at[GLA_RANK:2 * GLA_RANK, GLA_KW:].set(gla_gk_up_b[li]).astype(BF16)
    gk_bias = jnp.concatenate([gla_gk_bias_f[li], gla_gk_bias_b[li]])[None, :]
    n1 = norm1_w[li][None, :]
    dec_f = jnp.zeros((8, LANES), F32).at[:RET_HEADS].set(jnp.broadcast_to(ret_decay_f[li][:, None], (RET_HEADS, LANES)))
    dec_b = jnp.zeros((8, LANES), F32).at[:RET_HEADS].set(jnp.broadcast_to(ret_decay_b[li][:, None], (RET_HEADS, LANES)))
    cos, sin = _rope_tables(l // GRID_W, GRID_W)
    cos_c = jnp.ones((lc, RET_DK), F32)
    sin_c = jnp.zeros((lc, RET_DK), F32)

    fc = _inproj(ctx, csh1, csc1, n1, w_main, gk_up, gk_bias, cos_c, sin_c, tm=256)
    zr = jnp.zeros((b, RET_HEADS, RET_DK, RET_DK), F32)
    zg = jnp.zeros((b, GLA_HEADS, GLA_DK, GLA_DV), F32)
    ctx_out = _scan((fc[0], fc[1], fc[2], fc[4], fc[5], fc[6], fc[8], fc[9]), dec_f, dec_b,
                    (zr, zr, zg, zg), tb=256)
    ctx_states = ctx_out[4:]

    fl = _inproj(x, sh1, sc1, n1, w_main, gk_up, gk_bias, cos, sin, tm=256)
    lat = _scan((fl[0], fl[1], fl[2], fl[4], fl[5], fl[6], fl[8], fl[9]), dec_f, dec_b,
                ctx_states, tb=256)

    x1, h2t = _mix(lat[0], lat[1], lat[2], lat[3], fl[3], fl[7], x, g1, sh2, sc2,
                   norm2_w[li][None, :], w_out[li].astype(BF16), tm=256)

    wq_t = peer_w_q[li].T.astype(BF16)
    return _peer(h2t, wq_t, peer_k1[li].astype(BF16), peer_k2[li].astype(BF16),
                 peer_down[li].astype(BF16), peer_up[li].T.astype(BF16), x1, g2, norm_f_w[None, :],
                 tm=512, ib=16)
```

```python
import functools

import jax
import jax.numpy as jnp
from jax import lax
from jax.experimental import pallas as pl
from jax.experimental.pallas import tpu as pltpu

F32 = jnp.float32
BF16 = jnp.bfloat16
HIGHEST = lax.Precision.HIGHEST

D_MODEL = 1024
GRID_W = 64
RET_HEADS = 4
RET_WIDTH = 512
RET_DK = 128
GLA_HEADS = 4
GLA_WIDTH = 512
GLA_DV = 128
GLA_DK = 64
GLA_KW = GLA_HEADS * GLA_DK
GLA_RANK = 16
GLA_GATE_NORM = 16.0
CHUNK = 64
LOG2_CHUNK = CHUNK.bit_length() - 1
ROPE_BASE = 10000.0
N_KEYS = 128
PEER_HEADS = 8
PEER_TOPK = 16
PEER_DQ = 256
N_MOD = 6
EPS = 1e-6
MAIN_COLS = 3584
LANES = 128
SUBLANES = 8
PACK16 = 16
MXU_DEPTH = 256
VMEM_LIMIT = 56 * 1024 * 1024


def _cparams(sem):
    return pltpu.CompilerParams(dimension_semantics=sem, vmem_limit_bytes=VMEM_LIMIT)


def _silu(v):
    return v / (1.0 + jnp.exp(-v))


def _log_sigmoid(z):
    return jnp.minimum(z, 0.0) - jnp.log(1.0 + jnp.exp(-jnp.abs(z)))


def _dot(a, b):
    return jnp.dot(a, b, preferred_element_type=F32)


def _dot_nt(a, b):
    return lax.dot_general(a, b, (((1,), (1,)), ((), ())), preferred_element_type=F32)


def _mod_kernel(c_ref, w_ref, b_ref, o_ref):
    a = _silu(c_ref[...])
    o_ref[...] = jnp.dot(a, w_ref[...], precision=HIGHEST, preferred_element_type=F32) + b_ref[...]


def _modulation(cc, w_mod, b_mod):
    n = w_mod.shape[1]
    tn = 1536
    return pl.pallas_call(
        _mod_kernel,
        grid=(n // tn,),
        in_specs=[pl.BlockSpec((8, D_MODEL), lambda j: (0, 0)),
                  pl.BlockSpec((D_MODEL, tn), lambda j: (0, j)),
                  pl.BlockSpec((1, tn), lambda j: (0, j))],
        out_specs=pl.BlockSpec((8, tn), lambda j: (0, j)),
        out_shape=jax.ShapeDtypeStruct((8, n), F32),
        compiler_params=_cparams(("arbitrary",)),
        name="modulation",
    )(cc, w_mod, b_mod)


def _inproj_kernel(x_ref, sh_ref, sc_ref, nw_ref, wm_ref, up_ref, gb_ref, cos_ref, sin_ref,
                   rq_ref, rk_ref, rv_ref, rg_ref, gq_ref, gk_ref, gv_ref, gg_ref, laf_ref, lab_ref):
    x = x_ref[0]
    ms = jnp.mean(x * x, axis=-1, keepdims=True)
    h = x * lax.rsqrt(ms + EPS) * nw_ref[...]
    h = h * (1.0 + sc_ref[0]) + sh_ref[0]
    p = _dot(h.astype(BF16), wm_ref[...])
    cos = cos_ref[...]
    sin = sin_ref[...]
    k_scale = RET_DK ** -0.5
    for hh in range(RET_HEADS):
        lo = hh * RET_DK
        q = p[:, lo:lo + RET_DK]
        k = p[:, RET_WIDTH + lo:RET_WIDTH + lo + RET_DK] * k_scale
        rq_ref[0, :, lo:lo + RET_DK] = q * cos + pltpu.roll(q, RET_DK // 2, axis=1) * sin
        rk_ref[0, :, lo:lo + RET_DK] = k * cos + pltpu.roll(k, RET_DK // 2, axis=1) * sin
    rv_ref[0] = p[:, 1024:1536]
    rg_ref[0] = p[:, 1536:2048]
    gq_ref[0] = p[:, 2048:2304] * (GLA_DK ** -0.5)
    gk_ref[0] = p[:, 2304:2560]
    gv_ref[0] = p[:, 2560:3072]
    gg_ref[0] = p[:, 3072:3584]
    lr = p[:, MAIN_COLS:MAIN_COLS + LANES]
    pre = _dot(lr.astype(BF16), up_ref[...]) + gb_ref[...]
    la = _log_sigmoid(pre) * (1.0 / GLA_GATE_NORM)
    laf_ref[0] = la[:, :GLA_KW]
    lab_ref[0] = la[:, GLA_KW:]


def _inproj(x, shift, scale, norm_w, w_main, gk_up, gk_bias, cos, sin, tm):
    b, l, d = x.shape
    nt = l // tm
    tok = lambda w: pl.BlockSpec((1, tm, w), lambda bi, i: (bi, i, 0))
    full = lambda shp: pl.BlockSpec(shp, lambda bi, i: tuple(0 for _ in shp))
    vec = pl.BlockSpec((1, 1, d), lambda bi, i: (bi, 0, 0))
    widths = (RET_WIDTH, RET_WIDTH, RET_WIDTH, RET_WIDTH, GLA_KW, GLA_KW, GLA_WIDTH, GLA_WIDTH, GLA_KW, GLA_KW)
    return pl.pallas_call(
        _inproj_kernel,
        grid=(b, nt),
        in_specs=[tok(d), vec, vec, full((1, d)), full((d, MAIN_COLS + LANES)),
                  full((LANES, 2 * GLA_KW)), full((1, 2 * GLA_KW)),
                  pl.BlockSpec((tm, RET_DK), lambda bi, i: (i, 0)),
                  pl.BlockSpec((tm, RET_DK), lambda bi, i: (i, 0))],
        out_specs=[tok(w) for w in widths],
        out_shape=[jax.ShapeDtypeStruct((b, l, w), F32) for w in widths],
        compiler_params=_cparams(("arbitrary", "arbitrary")),
        name="inproj",
    )(x, shift, scale, norm_w, w_main, gk_up, gk_bias, cos, sin)


def _scan_kernel(decf_ref, decb_ref,
                 rqf_ref, rkf_ref, rvf_ref, rqb_ref, rkb_ref, rvb_ref,
                 gqf_ref, gkf_ref, gvf_ref, laf_ref, gqb_ref, gkb_ref, gvb_ref, lab_ref,
                 s0rf_ref, s0rb_ref, s0gf_ref, s0gb_ref,
                 orf_ref, orb_ref, ogf_ref, ogb_ref,
                 srf_out, srb_out, sgf_out, sgb_out,
                 srf, srb, sgf, sgb, *, nch):
    i = pl.program_id(1)

    @pl.when(i == 0)
    def _():
        srf[...] = s0rf_ref[0]
        srb[...] = s0rb_ref[0]
        sgf[...] = s0gf_ref[0]
        sgb[...] = s0gb_ref[0]

    C = CHUNK

    lgf = _log_sigmoid(decf_ref[...])
    lgb = _log_sigmoid(decb_ref[...])

    CR = nch * C
    rrow = lax.broadcasted_iota(jnp.int32, (CR, CR), 0)
    rcol = lax.broadcasted_iota(jnp.int32, (CR, CR), 1)
    pos_r = lax.broadcasted_iota(jnp.int32, (CR, LANES), 0).astype(F32)

    def ret_consts(lg, h, fwd):
        lrow = lg[h:h + 1, :]
        lwide = jnp.concatenate([lrow] * (CR // LANES), axis=1)
        rel = ((rrow - rcol) if fwd else (rcol - rrow)).astype(F32)
        mask = jnp.where(rel >= 0, jnp.exp(lwide * jnp.maximum(rel, 0.0)), 0.0)
        if fwd:
            qd = jnp.exp(lrow * (pos_r + 1.0))
            kd = jnp.exp(lrow * (CR - 1.0 - pos_r))
        else:
            qd = jnp.exp(lrow * (CR - pos_r))
            kd = jnp.exp(lrow * pos_r)
        cd = jnp.exp(lrow * float(CR))
        return mask, qd, kd, cd

    def ret_block(q_ref, k_ref, v_ref, o_ref, s_ref, lg, fwd):
        for h in range(RET_HEADS):
            mask, qd, kd, cd = ret_consts(lg, h, fwd)
            lo = h * RET_DK
            q = q_ref[0, :, lo:lo + RET_DK]
            k = k_ref[0, :, lo:lo + RET_DK]
            v = v_ref[0, :, lo:lo + RET_DK].astype(BF16)
            s = s_ref[h]
            scores = _dot_nt(q.astype(BF16), k.astype(BF16)) * mask
            o = _dot(scores.astype(BF16), v) + _dot((q * qd).astype(BF16), s.astype(BF16))
            o_ref[0, :, lo:lo + RET_DK] = o
            s_ref[h] = cd * s + _dot((k * kd).T.astype(BF16), v)

    def gla_block(q_ref, k_ref, v_ref, a_ref, o_ref, s_ref, fwd):
        same = jnp.right_shift(rrow, LOG2_CHUNK) == jnp.right_shift(rcol, LOG2_CHUNK)
        keep = same & ((rrow >= rcol) if fwd else (rcol >= rrow))
        a = a_ref[0]
        g = jnp.dot(jnp.where(keep, 1.0, 0.0).astype(F32), a, precision=HIGHEST,
                    preferred_element_type=F32)
        qt = q_ref[0] * jnp.exp(g)
        kk = k_ref[0]
        kt = kk * jnp.exp(-g)
        kls, tots = [], []
        for c in range(nch):
            g_c = g[c * C:(c + 1) * C, :]
            g_last = g_c[C - 1:C, :] if fwd else g_c[0:1, :]
            kls.append(kk[c * C:(c + 1) * C, :] * jnp.exp(g_last - g_c))
            tots.append(jnp.exp(jnp.sum(a[c * C:(c + 1) * C, :].T, axis=1, keepdims=True)))
        order = range(nch) if fwd else range(nch - 1, -1, -1)
        for h in range(GLA_HEADS):
            lo = h * GLA_DK
            vo = h * GLA_DV
            v = v_ref[0, :, vo:vo + GLA_DV].astype(BF16)
            qh = qt[:, lo:lo + GLA_DK].astype(BF16)
            scores = jnp.where(keep, _dot_nt(qh, kt[:, lo:lo + GLA_DK].astype(BF16)), 0.0)
            o = _dot(scores.astype(BF16), v)
            upd = [_dot(kls[c][:, lo:lo + GLA_DK].T.astype(BF16), v[c * C:(c + 1) * C, :]) for c in range(nch)]
            s = s_ref[h]
            starts = [None] * nch
            for c in order:
                starts[c] = s
                s = tots[c][lo:lo + GLA_DK, :] * s + upd[c]
            s_ref[h] = s
            inter = [_dot(qh[c * C:(c + 1) * C, :], starts[c].astype(BF16)) for c in range(nch)]
            o_ref[0, :, vo:vo + GLA_DV] = o + jnp.concatenate(inter, axis=0)

    ret_block(rqf_ref, rkf_ref, rvf_ref, orf_ref, srf, lgf, True)
    ret_block(rqb_ref, rkb_ref, rvb_ref, orb_ref, srb, lgb, False)
    gla_block(gqf_ref, gkf_ref, gvf_ref, laf_ref, ogf_ref, sgf, True)
    gla_block(gqb_ref, gkb_ref, gvb_ref, lab_ref, ogb_ref, sgb, False)

    @pl.when(i == pl.num_programs(1) - 1)
    def _():
        srf_out[0] = srf[...]
        srb_out[0] = srb[...]
        sgf_out[0] = sgf[...]
        sgb_out[0] = sgb[...]


def _scan(feats, dec_f, dec_b, states, tb):
    rq, rk, rv, gq, gk, gv, la_f, la_b = feats
    b, l, _ = rq.shape
    nb = l // tb
    fwd = lambda w: pl.BlockSpec((1, tb, w), lambda bi, i: (bi, i, 0))
    bwd = lambda w: pl.BlockSpec((1, tb, w), lambda bi, i: (bi, nb - 1 - i, 0))
    dec = pl.BlockSpec((8, LANES), lambda bi, i: (0, 0))
    st_r = pl.BlockSpec((1, RET_HEADS, RET_DK, RET_DK), lambda bi, i: (bi, 0, 0, 0))
    st_g = pl.BlockSpec((1, GLA_HEADS, GLA_DK, GLA_DV), lambda bi, i: (bi, 0, 0, 0))
    o_shape = jax.ShapeDtypeStruct((b, l, RET_WIDTH), F32)
    sr_shape = jax.ShapeDtypeStruct((b, RET_HEADS, RET_DK, RET_DK), F32)
    sg_shape = jax.ShapeDtypeStruct((b, GLA_HEADS, GLA_DK, GLA_DV), F32)
    return pl.pallas_call(
        functools.partial(_scan_kernel, nch=tb // CHUNK),
        grid=(b, nb),
        in_specs=[dec, dec,
                  fwd(RET_WIDTH), fwd(RET_WIDTH), fwd(RET_WIDTH),
                  bwd(RET_WIDTH), bwd(RET_WIDTH), bwd(RET_WIDTH),
                  fwd(GLA_KW), fwd(GLA_KW), fwd(GLA_WIDTH), fwd(GLA_KW),
                  bwd(GLA_KW), bwd(GLA_KW), bwd(GLA_WIDTH), bwd(GLA_KW),
                  st_r, st_r, st_g, st_g],
        out_specs=[fwd(RET_WIDTH), bwd(RET_WIDTH), fwd(GLA_WIDTH), bwd(GLA_WIDTH),
                   st_r, st_r, st_g, st_g],
        out_shape=[o_shape, o_shape, o_shape, o_shape, sr_shape, sr_shape, sg_shape, sg_shape],
        scratch_shapes=[pltpu.VMEM((RET_HEADS, RET_DK, RET_DK), F32),
                        pltpu.VMEM((RET_HEADS, RET_DK, RET_DK), F32),
                        pltpu.VMEM((GLA_HEADS, GLA_DK, GLA_DV), F32),
                        pltpu.VMEM((GLA_HEADS, GLA_DK, GLA_DV), F32)],
        compiler_params=_cparams(("arbitrary", "arbitrary")),
        name="bidir_scan",
    )(dec_f, dec_b, rq, rk, rv, rq, rk, rv, gq, gk, gv, la_f, gq, gk, gv, la_b, *states)


def _mix_kernel(orf_ref, orb_ref, ogf_ref, ogb_ref, rg_ref, gg_ref, x_ref, g1_ref, sh_ref, sc_ref,
                nw_ref, wo_ref, x1_ref, h2t_ref):
    def head_norm(o):
        parts = []
        for h in range(RET_HEADS):
            oh = o[:, h * 128:(h + 1) * 128]
            parts.append(oh * lax.rsqrt(jnp.mean(oh * oh, axis=-1, keepdims=True) + EPS))
        return jnp.concatenate(parts, axis=-1)

    ret = head_norm(orf_ref[0] + orb_ref[0]) * _silu(rg_ref[0])
    gla = head_norm(ogf_ref[0] + ogb_ref[0]) * _silu(gg_ref[0])
    y = jnp.concatenate([ret, gla], axis=-1).astype(BF16)
    x1 = x_ref[0] + g1_ref[0] * _dot(y, wo_ref[...])
    x1_ref[0] = x1
    ms = jnp.mean(x1 * x1, axis=-1, keepdims=True)
    h2 = x1 * lax.rsqrt(ms + EPS) * nw_ref[...]
    h2 = h2 * (1.0 + sc_ref[0]) + sh_ref[0]
    h2t_ref[...] = h2.T.astype(BF16)


def _mix(o_rf, o_rb, o_gf, o_gb, rg, gg, x, g1, sh2, sc2, norm_w, w_out, tm):
    b, l, d = x.shape
    nt = l // tm
    tok = lambda w: pl.BlockSpec((1, tm, w), lambda bi, i: (bi, i, 0))
    vec = pl.BlockSpec((1, 1, d), lambda bi, i: (bi, 0, 0))
    return pl.pallas_call(
        _mix_kernel,
        grid=(b, nt),
        in_specs=[tok(512)] * 6 + [tok(d), vec, vec, vec,
                                   pl.BlockSpec((1, d), lambda bi, i: (0, 0)),
                                   pl.BlockSpec((d, d), lambda bi, i: (0, 0))],
        out_specs=[tok(d), pl.BlockSpec((d, tm), lambda bi, i: (0, bi * nt + i))],
        out_shape=[jax.ShapeDtypeStruct((b, l, d), F32), jax.ShapeDtypeStruct((d, b * l), BF16)],
        compiler_params=_cparams(("arbitrary", "arbitrary")),
        name="mix_outproj",
    )(o_rf, o_rb, o_gf, o_gb, rg, gg, x, g1, sh2, sc2, norm_w, w_out)


def _oddeven_merge_sort_pairs(n):
    pairs = []
    p = 1
    while p < n:
        k = p
        while k >= 1:
            for j in range(k % p, n - k, 2 * k):
                for i in range(min(k, n - j - k)):
                    if (i + j) // (2 * p) == (i + j + k) // (2 * p):
                        pairs.append((i + j, i + j + k))
            k //= 2
        p *= 2
    return pairs


def _bitonic_merge_pairs(n):
    pairs = []
    d = n // 2
    while d >= 1:
        pairs.extend((i, i + d) for i in range(n) if (i & d) == 0)
        d //= 2
    return pairs


_SORT16 = _oddeven_merge_sort_pairs(PEER_TOPK)
_MERGE16 = _bitonic_merge_pairs(PEER_TOPK)


def _compare_exchange(x, pairs):
    for a, b in pairs:
        hi = jnp.maximum(x[a], x[b])
        lo = jnp.minimum(x[a], x[b])
        x[a] = hi
        x[b] = lo


def _sublane_allreduce(v, op):
    for sh in (4, 2, 1):
        v = op(v, pltpu.roll(v, sh, axis=0))
    return v


def _top16_desc(s):
    x = [s[SUBLANES * k:SUBLANES * (k + 1), :] for k in range(PEER_TOPK)]
    _compare_exchange(x, _SORT16)
    for sh in (4, 2, 1):
        c = [jnp.maximum(x[k], pltpu.roll(x[PEER_TOPK - 1 - k], sh, axis=0)) for k in range(PEER_TOPK)]
        _compare_exchange(c, _MERGE16)
        x = c
    return x


def _route_group(s1, s2):
    inf = jnp.inf
    v1 = _top16_desc(s1)
    v2 = _top16_desc(s2)
    sub = lax.broadcasted_iota(jnp.int32, v1[0].shape, 0)

    def pack_rows(v):
        out = v[0]
        for r in range(1, SUBLANES):
            out = jnp.where(sub == r, v[r], out)
        return out

    v1lo, v1hi = pack_rows(v1[:SUBLANES]), pack_rows(v1[SUBLANES:])
    v2lo, v2hi = pack_rows(v2[:SUBLANES]), pack_rows(v2[SUBLANES:])
    cands = [v1[0] + v2lo, v1[0] + v2hi] + [v1[a] + v2lo for a in range(1, SUBLANES)] + [v1hi + v2[0]]
    c = list(cands)
    tau = None
    for _ in range(PEER_TOPK):
        m = c[0]
        for ci in c[1:]:
            m = jnp.maximum(m, ci)
        tau = _sublane_allreduce(m, jnp.maximum)
        c = [jnp.where(ci == tau, -inf, ci) for ci in c]
    m12 = v1[0] + v2[0]
    zs = None
    for cd in cands:
        t = jnp.where(cd >= tau, jnp.exp(cd - m12), 0.0)
        zs = t if zs is None else zs + t
    zinv = 1.0 / _sublane_allreduce(zs, jnp.add)
    thetas = []
    for b in range(PEER_TOPK):
        lo = jnp.where(v1lo + v2[b] >= tau, v1lo, inf)
        hi = jnp.where(v1hi + v2[b] >= tau, v1hi, inf)
        thetas.append(_sublane_allreduce(jnp.minimum(lo, hi), jnp.minimum))
    cnt, e1, rank, e2 = [], [], [], []
    for k in range(N_KEYS // SUBLANES):
        x = s1[SUBLANES * k:SUBLANES * (k + 1), :]
        y = s2[SUBLANES * k:SUBLANES * (k + 1), :]
        ck = jnp.where(x >= thetas[0], 1.0, 0.0)
        rk = jnp.where(v2[0] > y, 1.0, 0.0)
        for b in range(1, PEER_TOPK):
            ck = ck + jnp.where(x >= thetas[b], 1.0, 0.0)
            rk = rk + jnp.where(v2[b] > y, 1.0, 0.0)
        cnt.append(ck)
        rank.append(rk)
        e1.append(jnp.exp(x - v1[0]) * zinv)
        e2.append(jnp.exp(y - v2[0]))
    return cnt, e1, rank, e2


def _peer_kernel(h2t_ref, wqt_ref, k1_ref, k2_ref, down_ref, upt_ref, x1_ref, g2_ref, nf_ref,
                 out_ref,
                 rank2_s, e2_s, cnt_s, e1_s, acc_s, *, ib):
    e = pl.program_id(1)
    tm = h2t_ref.shape[1]
    half = PEER_DQ // 2
    nk2 = N_KEYS // PACK16

    @pl.when(e == 0)
    def _():
        acc_s[...] = jnp.zeros_like(acc_s)

        def head_body(h, carry):
            ht = h2t_ref[...]
            row0 = pl.multiple_of(h * PEER_DQ, PEER_DQ)
            q = _dot(wqt_ref[pl.ds(row0, PEER_DQ), :], ht).astype(BF16)
            s1 = _dot(k1_ref[h], q[:half, :])
            s2 = _dot(k2_ref[h], q[half:, :])
            for g in range(tm // LANES):
                sl = slice(g * LANES, (g + 1) * LANES)
                cnt, e1, rank, e2 = _route_group(s1[:, sl], s2[:, sl])
                cnt_s[h, :, sl] = jnp.concatenate(cnt, axis=0)
                e1_s[h, :, sl] = jnp.concatenate(e1, axis=0)
                for k2 in range(nk2):
                    rank2_s[h, k2, :, sl] = jnp.concatenate(rank[2 * k2:2 * k2 + 2], axis=0).astype(BF16)
                    e2_s[h, k2, :, sl] = jnp.concatenate(e2[2 * k2:2 * k2 + 2], axis=0).astype(BF16)
            return carry

        lax.fori_loop(0, PEER_HEADS, head_body, 0)

    a_t = _dot(down_ref[...], h2t_ref[...])
    hs = []
    for ii in range(ib):
        i = e * ib + ii
        w = None
        for h in range(PEER_HEADS):
            c_row = jnp.broadcast_to(cnt_s[h, pl.ds(i, 1), :], (PACK16, tm)).astype(BF16)
            e_row = jnp.broadcast_to(e1_s[h, pl.ds(i, 1), :], (PACK16, tm)).astype(BF16)
            term = jnp.where(rank2_s[h] < c_row[None], e2_s[h], jnp.zeros((), BF16)) * e_row[None]
            w = term if w is None else w + term
        a = a_t[ii * N_KEYS:(ii + 1) * N_KEYS, :]
        act = 0.5 * a * (1.0 + lax.erf(a * (2.0 ** -0.5)))
        hs.append((w * act.astype(BF16).reshape(nk2, PACK16, tm)).reshape(N_KEYS, tm))
    acc_s[...] += _dot(upt_ref[...], jnp.concatenate(hs, axis=0))

    @pl.when(e == pl.num_programs(1) - 1)
    def _():
        y = x1_ref[0] + g2_ref[0] * acc_s[...].T
        ms = jnp.mean(y * y, axis=-1, keepdims=True)
        out_ref[0] = y * lax.rsqrt(ms + EPS) * nf_ref[...]


def _peer(h2t, wq_t, k1, k2, down, up_t, x1, g2, norm_f, tm, ib):
    b, l, d = x1.shape
    nt = l // tm
    ne = N_KEYS // ib
    te = ib * N_KEYS
    const = lambda shp: pl.BlockSpec(shp, lambda t, e: tuple(0 for _ in shp))
    return pl.pallas_call(
        functools.partial(_peer_kernel, ib=ib),
        grid=(b * nt, ne),
        in_specs=[pl.BlockSpec((d, tm), lambda t, e: (0, t)),
                  const((PEER_HEADS * PEER_DQ, d)),
                  const((PEER_HEADS, N_KEYS, PEER_DQ // 2)),
                  const((PEER_HEADS, N_KEYS, PEER_DQ // 2)),
                  pl.BlockSpec((te, d), lambda t, e: (e, 0)),
                  pl.BlockSpec((d, te), lambda t, e: (0, e)),
                  pl.BlockSpec((1, tm, d), lambda t, e: (t // nt, t % nt, 0)),
                  pl.BlockSpec((1, 1, d), lambda t, e: (t // nt, 0, 0)),
                  const((1, d))],
        out_specs=pl.BlockSpec((1, tm, d), lambda t, e: (t // nt, t % nt, 0)),
        out_shape=jax.ShapeDtypeStruct((b, l, d), F32),
        scratch_shapes=[pltpu.VMEM((PEER_HEADS, N_KEYS // PACK16, PACK16, tm), BF16),
                        pltpu.VMEM((PEER_HEADS, N_KEYS // PACK16, PACK16, tm), BF16),
                        pltpu.VMEM((PEER_HEADS, N_KEYS, tm), F32),
                        pltpu.VMEM((PEER_HEADS, N_KEYS, tm), F32),
                        pltpu.VMEM((d, tm), F32)],
        compiler_params=_cparams(("arbitrary", "arbitrary")),
        name="peer",
    )(h2t, wq_t, k1, k2, down, up_t, x1, g2, norm_f)


def _rope_tables(rows, cols):
    r, c = jnp.meshgrid(jnp.arange(rows, dtype=F32), jnp.arange(cols, dtype=F32), indexing="ij")
    n_freq = RET_DK // 4
    inv = ROPE_BASE ** (-jnp.arange(n_freq, dtype=F32) / n_freq)
    ang = jnp.concatenate([r.reshape(-1, 1) * inv, c.reshape(-1, 1) * inv], axis=-1)
    cos, sin = jnp.cos(ang), jnp.sin(ang)
    return jnp.concatenate([cos, cos], axis=-1), jnp.concatenate([-sin, sin], axis=-1)


def kernel(x, c, ctx, c_ctx, w_mod, b_mod, norm1_w, norm2_w, w_in, ret_decay_f, ret_decay_b,
           gla_gk_up_f, gla_gk_bias_f, gla_gk_up_b, gla_gk_bias_b, w_out, peer_w_q, peer_k1, peer_k2,
           peer_down, peer_up, norm_f_w):
    depth = w_mod.shape[0]
    assert depth == 1
    b, l, d = x.shape
    lc = ctx.shape[1]
    li = 0

    cc = jnp.zeros((8, d), F32).at[:b].set(c).at[b].set(c_ctx)
    mod = _modulation(cc, w_mod[li], b_mod[li][None, :])
    sh1, sc1, g1, sh2, sc2, g2 = [mod[:b, j * d:(j + 1) * d][:, None, :] for j in range(N_MOD)]
    csh1, csc1 = [jnp.broadcast_to(mod[b:b + 1, j * d:(j + 1) * d][:, None, :], (b, 1, d)) for j in range(2)]

    w_main = jnp.pad(w_in[li], ((0, 0), (0, MAIN_COLS + LANES - w_in.shape[2]))).astype(BF16)
    gk_up = jnp.zeros((LANES, 2 * GLA_KW), F32)
    gk_up = gk_up.at[:GLA_RANK, :GLA_KW].set(gla_gk_up_f[li])
    gk_up = gk_up.at[GLA_RANK:2 * GLA_RANK, GLA_KW:].set(gla_gk_up_b[li]).astype(BF16)
    gk_bias = jnp.concatenate([gla_gk_bias_f[li], gla_gk_bias_b[li]])[None, :]
    n1 = norm1_w[li][None, :]
    dec_f = jnp.zeros((8, LANES), F32).at[:RET_HEADS].set(jnp.broadcast_to(ret_decay_f[li][:, None], (RET_HEADS, LANES)))
    dec_b = jnp.zeros((8, LANES), F32).at[:RET_HEADS].set(jnp.broadcast_to(ret_decay_b[li][:, None], (RET_HEADS, LANES)))
    cos, sin = _rope_tables(l // GRID_W, GRID_W)
    cos_c = jnp.ones((lc, RET_DK), F32)
    sin_c = jnp.zeros((lc, RET_DK), F32)

    fc = _inproj(ctx, csh1, csc1, n1, w_main, gk_up, gk_bias, cos_c, sin_c, tm=256)
    zr = jnp.zeros((b, RET_HEADS, RET_DK, RET_DK), F32)
    zg = jnp.zeros((b, GLA_HEADS, GLA_DK, GLA_DV), F32)
    ctx_out = _scan((fc[0], fc[1], fc[2], fc[4], fc[5], fc[6], fc[8], fc[9]), dec_f, dec_b,
                    (zr, zr, zg, zg), tb=256)
    ctx_states = ctx_out[4:]

    fl = _inproj(x, sh1, sc1, n1, w_main, gk_up, gk_bias, cos, sin, tm=256)
    lat = _scan((fl[0], fl[1], fl[2], fl[4], fl[5], fl[6], fl[8], fl[9]), dec_f, dec_b,
                ctx_states, tb=256)

    x1, h2t = _mix(lat[0], lat[1], lat[2], lat[3], fl[3], fl[7], x, g1, sh2, sc2,
                   norm2_w[li][None, :], w_out[li].astype(BF16), tm=256)

    wq_t = peer_w_q[li].T.astype(BF16)
    return _peer(h2t, wq_t, peer_k1[li].astype(BF16), peer_k2[li].astype(BF16),
                 peer_down[li].astype(BF16), peer_up[li].T.astype(BF16), x1, g2, norm_f_w[None, :],
                 tm=512, ib=16)
```
